```python
import jax, jax.numpy as jnp
from jax import lax
import numpy as np

D_MODEL = 1024
BATCH = 32
SEQ = 2048
DEPTH = 1

CHUNK = 64
Q_BLOCK = 128
N_HEADS = 8
QK_NOPE_DIM = 64
QK_ROPE_DIM = 32
QK_HEAD_DIM = QK_NOPE_DIM + QK_ROPE_DIM
V_HEAD_DIM = 64
Q_LORA_RANK = 256
KV_LORA_RANK = 128
MLA_WIDTH = N_HEADS * V_HEAD_DIM
CONV_CH = 512
CONV_WIDTH = 31
D_FF = 4 * D_MODEL
N_BRANCH = 2
ADA_CHUNKS = 6
ROPE_THETA = 10000.0
EPS = 1e-6

OFF_Q = Q_LORA_RANK
OFF_KV = OFF_Q + KV_LORA_RANK
OFF_KR = OFF_KV + QK_ROPE_DIM
OFF_GLU = OFF_KR + 2 * CONV_CH
D_IN = OFF_GLU + N_BRANCH * D_MODEL

kernel_name = "hybrid_mla_conformer_conv_adaln_block"


def rms_norm(x, g):
    xf = x.astype(jnp.float32)
    y = xf * lax.rsqrt(jnp.mean(jnp.square(xf), axis=-1, keepdims=True) + EPS)
    return (y * g.astype(jnp.float32)).astype(x.dtype)


def layer_norm(x, g, b):
    xf = x.astype(jnp.float32)
    mu = jnp.mean(xf, axis=-1, keepdims=True)
    var = jnp.mean(jnp.square(xf - mu), axis=-1, keepdims=True)
    y = (xf - mu) * lax.rsqrt(var + EPS)
    return (y * g.astype(jnp.float32) + b.astype(jnp.float32)).astype(x.dtype)


def rope_tables(seq, dtype):
    inv_freq = ROPE_THETA ** (-jnp.arange(0, QK_ROPE_DIM, 2, dtype=jnp.float32) / QK_ROPE_DIM)
    ang = jnp.arange(seq, dtype=jnp.float32)[:, None] * inv_freq[None, :]
    return jnp.cos(ang)[:, None, :].astype(dtype), jnp.sin(ang)[:, None, :].astype(dtype)


def apply_rope(x, cos, sin):
    half = x.shape[-1] // 2
    x1, x2 = x[..., :half], x[..., half:]
    return jnp.concatenate([x1 * cos - x2 * sin, x2 * cos + x1 * sin], axis=-1)


def chunk_causal_attention(q, k, v):
    seq = q.shape[1]
    scale = QK_HEAD_DIM ** -0.5
    chunk_id = jnp.arange(seq) // CHUNK
    outs = []
    for q0 in range(0, seq, Q_BLOCK):
        kv_end = q0 + Q_BLOCK
        qb = q[:, q0:kv_end]
        kb = k[:, :kv_end]
        vb = v[:, :kv_end]
        s = jnp.einsum('bqhd,bkhd->bhqk', qb, kb).astype(jnp.float32) * scale
        mask = chunk_id[q0:kv_end][:, None] >= chunk_id[:kv_end][None, :]
        s = jnp.where(mask[None, None], s, jnp.finfo(jnp.float32).min)
        p = jax.nn.softmax(s, axis=-1).astype(v.dtype)
        outs.append(jnp.einsum('bhqk,bkhd->bqhd', p, vb))
    return jnp.concatenate(outs, axis=1)


def causal_depthwise_conv(u, w, b):
    out = lax.conv_general_dilated(
        u, w[:, None, :].astype(u.dtype), window_strides=(1,),
        padding=[(CONV_WIDTH - 1, 0)],
        dimension_numbers=('NWC', 'WIO', 'NWC'),
        feature_group_count=u.shape[-1])
    return out + b


def setup_inputs(seed: int = 0) -> dict:
    key = jax.random.key(seed)
    ks = jax.random.split(key, 24)
    f32 = jnp.float32
    L = DEPTH

    def nrm(k, shape, fan_in):
        return jax.random.normal(k, shape, f32) * (fan_in ** -0.5)

    def gain(k, shape):
        return 1.0 + 0.02 * jax.random.normal(k, shape, f32)

    return {
        "x": jax.random.normal(ks[0], (BATCH, SEQ, D_MODEL), f32),
        "c": jax.random.normal(ks[1], (BATCH, D_MODEL), f32),
        "w_ada": nrm(ks[2], (L, D_MODEL, ADA_CHUNKS * D_MODEL), D_MODEL),
        "b_ada": 0.02 * jax.random.normal(ks[3], (L, ADA_CHUNKS * D_MODEL), f32),
        "norm1_g": gain(ks[4], (L, D_MODEL)),
        "w_in": nrm(ks[5], (L, D_MODEL, D_IN), D_MODEL),
        "q_latent_g": gain(ks[6], (L, Q_LORA_RANK)),
        "w_uq": nrm(ks[7], (L, Q_LORA_RANK, N_HEADS * QK_HEAD_DIM), Q_LORA_RANK),
        "kv_latent_g": gain(ks[8], (L, KV_LORA_RANK)),
        "w_ukv": nrm(ks[9], (L, KV_LORA_RANK, N_HEADS * (QK_NOPE_DIM + V_HEAD_DIM)), KV_LORA_RANK),
        "qk_norm_q_g": gain(ks[10], (L, QK_HEAD_DIM)),
        "qk_norm_k_g": gain(ks[11], (L, QK_HEAD_DIM)),
        "w_o_mla": nrm(ks[12], (L, MLA_WIDTH, D_MODEL), MLA_WIDTH),
        "conv_w": nrm(ks[13], (L, CONV_WIDTH, CONV_CH), CONV_WIDTH),
        "conv_b": 0.02 * jax.random.normal(ks[14], (L, CONV_CH), f32),
        "conv_ln_g": gain(ks[15], (L, CONV_CH)),
        "conv_ln_b": 0.02 * jax.random.normal(ks[16], (L, CONV_CH), f32),
        "w_pw_out": nrm(ks[17], (L, CONV_CH, D_MODEL), CONV_CH),
        "w_out": nrm(ks[18], (L, D_MODEL, D_MODEL), D_MODEL),
        "norm2_g": gain(ks[19], (L, D_MODEL)),
        "w_ff1": nrm(ks[20], (L, D_MODEL, D_FF), D_MODEL),
        "w_ff2": nrm(ks[21], (L, D_FF, D_MODEL), D_FF),
    }


def reference(x, c, w_ada, b_ada, norm1_g, w_in, q_latent_g, w_uq, kv_latent_g, w_ukv,
              qk_norm_q_g, qk_norm_k_g, w_o_mla, conv_w, conv_b, conv_ln_g, conv_ln_b,
              w_pw_out, w_out, norm2_g, w_ff1, w_ff2):
    B, S, D = x.shape
    cos, sin = rope_tables(S, x.dtype)
    c_act = jax.nn.silu(c)
    for l in range(DEPTH):
        mod = c_act @ w_ada[l] + b_ada[l]
        shift1, scale1, gate1, shift2, scale2, gate2 = jnp.split(mod[:, None, :], ADA_CHUNKS, axis=-1)

        h = rms_norm(x, norm1_g[l]) * (1.0 + scale1) + shift1
        z = h @ w_in[l]
        z_q = z[..., :OFF_Q]
        z_kv = z[..., OFF_Q:OFF_KV]
        z_kr = z[..., OFF_KV:OFF_KR]
        z_glu = z[..., OFF_KR:OFF_GLU]
        z_gate = z[..., OFF_GLU:]

        q = (rms_norm(z_q, q_latent_g[l]) @ w_uq[l]).reshape(B, S, N_HEADS, QK_HEAD_DIM)
        kv = (rms_norm(z_kv, kv_latent_g[l]) @ w_ukv[l]).reshape(B, S, N_HEADS, QK_NOPE_DIM + V_HEAD_DIM)
        k_nope, v = kv[..., :QK_NOPE_DIM], kv[..., QK_NOPE_DIM:]
        k_rope = jnp.broadcast_to(z_kr[:, :, None, :], (B, S, N_HEADS, QK_ROPE_DIM))
        k = jnp.concatenate([k_nope, k_rope], axis=-1)
        q = rms_norm(q, qk_norm_q_g[l])
        k = rms_norm(k, qk_norm_k_g[l])
        q = jnp.concatenate([q[..., :QK_NOPE_DIM], apply_rope(q[..., QK_NOPE_DIM:], cos, sin)], axis=-1)
        k = jnp.concatenate([k[..., :QK_NOPE_DIM], apply_rope(k[..., QK_NOPE_DIM:], cos, sin)], axis=-1)
        attn = chunk_causal_attention(q, k, v).reshape(B, S, MLA_WIDTH)
        y_a = attn @ w_o_mla[l]

        glu_a, glu_b = jnp.split(z_glu, 2, axis=-1)
        u = glu_a * jax.nn.sigmoid(glu_b)
        u = causal_depthwise_conv(u, conv_w[l], conv_b[l])
        u = jax.nn.silu(layer_norm(u, conv_ln_g[l], conv_ln_b[l]))
        y_b = u @ w_pw_out[l]

        g_a, g_b = jnp.split(jax.nn.sigmoid(z_gate), N_BRANCH, axis=-1)
        mixed = (g_a * y_a + g_b * y_b) @ w_out[l]
        x = x + gate1 * mixed

        h2 = rms_norm(x, norm2_g[l]) * (1.0 + scale2) + shift2
        f = jnp.square(jax.nn.relu(h2 @ w_ff1[l])) @ w_ff2[l]
        x = x + gate2 * f
    return x
```

```python
import functools
import math

import jax
import jax.numpy as jnp
from jax import lax
from jax.experimental import pallas as pl
from jax.experimental.pallas import tpu as pltpu

F32 = jnp.float32
BF16 = jnp.bfloat16

CHUNK = 64
N_HEADS = 8
QK_NOPE_DIM = 64
QK_ROPE_DIM = 32
QK_HEAD_DIM = QK_NOPE_DIM + QK_ROPE_DIM
V_HEAD_DIM = 64
Q_LORA_RANK = 256
KV_LORA_RANK = 128
CONV_CH = 512
CONV_WIDTH = 31
ADA_CHUNKS = 6
ROPE_THETA = 10000.0
EPS = 1e-6

LANES = 128
SUBLANES = 8
VMEM_LIMIT_BYTES = 56 * 1024 * 1024

HEAD_BLOCK = LANES
HALO = 32
ROPE_HALF = QK_ROPE_DIM // 2

PROJ_TILE = 512
FFN_TILE = 512
ATTN_QBLOCK = 256
CONV_ROWS = 32
FF_CHUNK = 1024


def _rms_scale(x, width):
    return lax.rsqrt(jnp.sum(x * x, axis=-1, keepdims=True) * (1.0 / width) + EPS)


def _adaln_kernel(c_ref, w_ref, b_ref, o_ref):
    c = c_ref[...]
    act = (c * jax.nn.sigmoid(c)).astype(BF16)
    o_ref[...] = jnp.dot(act, w_ref[...].astype(BF16), preferred_element_type=F32) + b_ref[...]


def _adaln(c, w, b):
    bsz, d = c.shape
    n = w.shape[1]
    tn = 1024
    return pl.pallas_call(
        _adaln_kernel,
        out_shape=jax.ShapeDtypeStruct((bsz, n), F32),
        grid=(n // tn,),
        in_specs=[
            pl.BlockSpec((bsz, d), lambda j: (0, 0)),
            pl.BlockSpec((d, tn), lambda j: (0, j)),
            pl.BlockSpec((1, tn), lambda j: (0, j)),
        ],
        out_specs=pl.BlockSpec((bsz, tn), lambda j: (0, j)),
        compiler_params=pltpu.CompilerParams(
            dimension_semantics=("arbitrary",), vmem_limit_bytes=VMEM_LIMIT_BYTES),
        name="adaln",
    )(c, w, b.reshape(1, n))


def _proj_kernel(x_ref, mod_ref, g1_ref, wa_ref, wglu_ref, wgate_ref, gq_ref, wuq_ref, wuqs_ref,
                 gkv_ref, wukv_ref, aq_ref, bq_ref, ak_ref, bk_ref, vone_ref,
                 q_ref, k_ref, v_ref, u_ref, gate_ref):
    d_qk = N_HEADS * HEAD_BLOCK
    x = x_ref[0]
    scale_row = g1_ref[...] * (1.0 + mod_ref[0, 1:2, :])
    shift_row = mod_ref[0, 0:1, :]
    hb = (x * _rms_scale(x, x.shape[-1]) * scale_row + shift_row).astype(BF16)

    z_a = jnp.dot(hb, wa_ref[...], preferred_element_type=F32)

    zq = z_a[:, :Q_LORA_RANK]
    qn = (zq * _rms_scale(zq, Q_LORA_RANK) * gq_ref[...]).astype(BF16)
    q_pre = jnp.dot(qn, wuq_ref[...], preferred_element_type=F32)
    q_swp = jnp.dot(qn, wuqs_ref[...], preferred_element_type=F32)
    aq = aq_ref[...]
    bq = bq_ref[...]
    for h in range(N_HEADS):
        sl = slice(h * HEAD_BLOCK, (h + 1) * HEAD_BLOCK)
        blk = q_pre[:, sl]
        qh = _rms_scale(blk, QK_HEAD_DIM) * (blk * aq + q_swp[:, sl] * bq)
        q_ref[0, :, sl] = qh.astype(BF16)

    zkv = z_a[:, Q_LORA_RANK:Q_LORA_RANK + KV_LORA_RANK]
    kvn = (zkv * _rms_scale(zkv, KV_LORA_RANK) * gkv_ref[...]).astype(BF16)
    kv_up = jnp.dot(kvn, wukv_ref[...], preferred_element_type=F32)

    kr_blk = z_a[:, Q_LORA_RANK + KV_LORA_RANK:]
    lane = lax.broadcasted_iota(jnp.int32, (1, LANES), 1)
    kr_only = jnp.where(lane < QK_ROPE_DIM, kr_blk, 0.0)
    ss_kr = jnp.sum(kr_only * kr_only, axis=-1, keepdims=True)
    ak = ak_ref[...]
    bk = bk_ref[...]
    kr_rot = (pltpu.roll(kr_blk, QK_NOPE_DIM, 1) * ak
              + pltpu.roll(kr_blk, QK_NOPE_DIM - QK_ROPE_DIM, 1) * bk)
    for h in range(N_HEADS):
        sl = slice(h * HEAD_BLOCK, (h + 1) * HEAD_BLOCK)
        blk = kv_up[:, sl]
        ss = jnp.sum(blk * blk, axis=-1, keepdims=True) + ss_kr
        kh = lax.rsqrt(ss * (1.0 / QK_HEAD_DIM) + EPS) * (blk * ak + kr_rot)
        k_ref[0, :, sl] = kh.astype(BF16)

    v_ref[0] = (kv_up[:, d_qk:] + vone_ref[...]).astype(BF16)

    zg = jnp.dot(hb, wglu_ref[...], preferred_element_type=F32)
    u_ref[0] = zg[:, :CONV_CH] * jax.nn.sigmoid(zg[:, CONV_CH:])

    gate_ref[0] = jax.nn.sigmoid(
        jnp.dot(hb, wgate_ref[...], preferred_element_type=F32)).astype(BF16)


def _const_spec(shape):
    nd = len(shape)
    return pl.BlockSpec(shape, lambda *_: (0,) * nd, pipeline_mode=pl.Buffered(1))


def _proj(x, mod, g1, wa, wglu, wgate, gq, wuq, wuqs, gkv, wukv, aq, bq, ak, bk, vone):
    bsz, seq, d = x.shape
    t = PROJ_TILE
    d_qk = N_HEADS * HEAD_BLOCK
    tok = lambda w: pl.BlockSpec((1, t, w), lambda b, s: (b, s, 0))
    pos = lambda: pl.BlockSpec((t, LANES), lambda b, s: (s, 0))
    out_shapes = (
        jax.ShapeDtypeStruct((bsz, seq, d_qk), BF16),
        jax.ShapeDtypeStruct((bsz, seq, d_qk), BF16),
        jax.ShapeDtypeStruct((bsz, seq, d_qk), BF16),
        jax.ShapeDtypeStruct((bsz, seq, CONV_CH), F32),
        jax.ShapeDtypeStruct((bsz, seq, 2 * d), BF16),
    )
    return pl.pallas_call(
        _proj_kernel,
        out_shape=out_shapes,
        grid=(bsz, seq // t),
        in_specs=[
            tok(d),
            pl.BlockSpec((1, ADA_CHUNKS, d), lambda b, s: (b, 0, 0)),
            _const_spec(g1.shape), _const_spec(wa.shape), _const_spec(wglu.shape),
            _const_spec(wgate.shape), _const_spec(gq.shape), _const_spec(wuq.shape),
            _const_spec(wuqs.shape), _const_spec(gkv.shape), _const_spec(wukv.shape),
            pos(), pos(), pos(), pos(), _const_spec(vone.shape),
        ],
        out_specs=(tok(d_qk), tok(d_qk), tok(d_qk), tok(CONV_CH), tok(2 * d)),
        compiler_params=pltpu.CompilerParams(
            dimension_semantics=("arbitrary", "arbitrary"), vmem_limit_bytes=VMEM_LIMIT_BYTES),
        name="proj",
    )(x, mod, g1, wa, wglu, wgate, gq, wuq, wuqs, gkv, wukv, aq, bq, ak, bk, vone)


def _attn_kernel(q_ref, k_ref, v_ref, o_ref):
    seq = q_ref.shape[1]
    qb = ATTN_QBLOCK
    neg = jnp.finfo(F32).min
    row_chunk = lax.broadcasted_iota(jnp.int32, (qb, qb), 0) // CHUNK
    col_chunk = lax.broadcasted_iota(jnp.int32, (qb, qb), 1) // CHUNK
    diag_mask = col_chunk <= row_chunk
    lane = lax.broadcasted_iota(jnp.int32, (qb, LANES), 1)
    contract_last = (((1,), (1,)), ((), ()))
    for j in range(seq // qb):
        q0 = j * qb
        q1 = q0 + qb
        outs = []
        for hh in range(2):
            sl = slice(hh * HEAD_BLOCK, (hh + 1) * HEAD_BLOCK)
            qt = q_ref[0, q0:q1, sl]
            s_d = lax.dot_general(qt, k_ref[0, q0:q1, sl], contract_last, preferred_element_type=F32)
            s_d = jnp.where(diag_mask, s_d, neg)
            m = jnp.max(s_d, axis=-1, keepdims=True)
            if j > 0:
                s_o = lax.dot_general(qt, k_ref[0, 0:q0, sl], contract_last, preferred_element_type=F32)
                m = jnp.maximum(m, jnp.max(s_o, axis=-1, keepdims=True))
            p_d = jnp.exp2(s_d - m).astype(BF16)
            o = jnp.dot(p_d, v_ref[0, q0:q1, sl], preferred_element_type=F32)
            if j > 0:
                p_o = jnp.exp2(s_o - m).astype(BF16)
                o = o + jnp.dot(p_o, v_ref[0, 0:q0, sl], preferred_element_type=F32)
            denom = o[:, V_HEAD_DIM:V_HEAD_DIM + 1] if hh == 0 else o[:, 0:1]
            outs.append(o / denom)
        o_ref[0, q0:q1, :] = jnp.where(lane < V_HEAD_DIM, outs[0], outs[1]).astype(BF16)


def _attn(q, k, v):
    bsz, seq, d_qk = q.shape
    pair = 2 * HEAD_BLOCK
    spec = pl.BlockSpec((1, seq, pair), lambda b, p: (b, 0, p))
    return pl.pallas_call(
        _attn_kernel,
        out_shape=jax.ShapeDtypeStruct((bsz, seq, N_HEADS * V_HEAD_DIM), BF16),
        grid=(bsz, d_qk // pair),
        in_specs=[spec, spec, spec],
        out_specs=pl.BlockSpec((1, seq, 2 * V_HEAD_DIM), lambda b, p: (b, 0, p)),
        compiler_params=pltpu.CompilerParams(
            dimension_semantics=("arbitrary", "arbitrary"), vmem_limit_bytes=VMEM_LIMIT_BYTES),
        name="attn",
    )(q, k, v)


def _ffn_kernel(x_ref, attn_ref, u_ref, halo_ref, gate_ref, mod_ref, wo_ref, cw_ref, cb_ref,
                lng_ref, lnb_ref, wpw_ref, wout_ref, g2_ref, w1_ref, w2_ref, o_ref, ubuf, cbuf):
    t = x_ref.shape[1]
    d = x_ref.shape[2]
    first = pl.program_id(1) == 0
    ext = t + HALO
    ubuf[0, 0:HALO, :] = jnp.where(first, 0.0, halo_ref[0])
    ubuf[0, HALO:, :] = u_ref[0]
    for p in range(1, SUBLANES):
        ubuf[p, 0:ext - SUBLANES, :] = ubuf[0, p:ext - SUBLANES + p, :]

    conv_bias = cb_ref[...]
    ln_g = lng_ref[...]
    ln_b = lnb_ref[...]
    tap0 = HALO - (CONV_WIDTH - 1)

    def conv_chunk(i, carry):
        r0 = pl.multiple_of(i * CONV_ROWS, CONV_ROWS)
        acc = jnp.broadcast_to(conv_bias, (CONV_ROWS, CONV_CH))
        for tap in range(CONV_WIDTH):
            blk, phase = divmod(tap0 + tap, SUBLANES)
            rows = pl.ds(pl.multiple_of(r0 + blk * SUBLANES, SUBLANES), CONV_ROWS)
            acc = acc + ubuf[phase, rows, :] * cw_ref[tap:tap + 1, :]
        mu = jnp.mean(acc, axis=-1, keepdims=True)
        cen = acc - mu
        var = jnp.mean(cen * cen, axis=-1, keepdims=True)
        y = cen * lax.rsqrt(var + EPS) * ln_g + ln_b
        cbuf[pl.ds(r0, CONV_ROWS), :] = (y * jax.nn.sigmoid(y)).astype(BF16)
        return carry

    lax.fori_loop(0, t // CONV_ROWS, conv_chunk, 0)

    y_a = jnp.dot(attn_ref[0], wo_ref[...], preferred_element_type=F32)
    y_b = jnp.dot(cbuf[...], wpw_ref[...], preferred_element_type=F32)
    g_a = gate_ref[0, :, :d].astype(F32)
    g_b = gate_ref[0, :, d:].astype(F32)
    mix = (g_a * y_a + g_b * y_b).astype(BF16)
    mixed = jnp.dot(mix, wout_ref[...], preferred_element_type=F32)

    x1 = x_ref[0] + mod_ref[0, 2:3, :] * mixed
    scale_row = g2_ref[...] * (1.0 + mod_ref[0, 4:5, :])
    h2 = (x1 * _rms_scale(x1, d) * scale_row + mod_ref[0, 3:4, :]).astype(BF16)

    f = jnp.zeros((t, d), F32)
    for c in range(w1_ref.shape[1] // FF_CHUNK):
        cs = slice(c * FF_CHUNK, (c + 1) * FF_CHUNK)
        hid = jnp.maximum(jnp.dot(h2, w1_ref[:, cs], preferred_element_type=F32), 0.0)
        f = f + jnp.dot((hid * hid).astype(BF16), w2_ref[cs, :], preferred_element_type=F32)
    o_ref[0] = x1 + mod_ref[0, 5:6, :] * f


def _ffn(x, attn, u, gates, mod, wo, cw, cb, lng, lnb, wpw, wout, g2, w1, w2):
    bsz, seq, d = x.shape
    t = FFN_TILE
    tok = lambda w: pl.BlockSpec((1, t, w), lambda b, s: (b, s, 0))
    halo_spec = pl.BlockSpec(
        (1, HALO, CONV_CH), lambda b, s: (b, jnp.maximum(s * (t // HALO) - 1, 0), 0))
    return pl.pallas_call(
        _ffn_kernel,
        out_shape=jax.ShapeDtypeStruct((bsz, seq, d), x.dtype),
        grid=(bsz, seq // t),
        in_specs=[
            tok(d), tok(attn.shape[-1]), tok(CONV_CH), halo_spec, tok(2 * d),
            pl.BlockSpec((1, ADA_CHUNKS, d), lambda b, s: (b, 0, 0)),
            _const_spec(wo.shape), _const_spec(cw.shape), _const_spec(cb.shape),
            _const_spec(lng.shape), _const_spec(lnb.shape), _const_spec(wpw.shape),
            _const_spec(wout.shape), _const_spec(g2.shape), _const_spec(w1.shape),
            _const_spec(w2.shape),
        ],
        out_specs=tok(d),
        scratch_shapes=[
            pltpu.VMEM((SUBLANES, t + HALO, CONV_CH), F32),
            pltpu.VMEM((t, CONV_CH), BF16),
        ],
        compiler_params=pltpu.CompilerParams(
            dimension_semantics=("arbitrary", "arbitrary"), vmem_limit_bytes=VMEM_LIMIT_BYTES),
        name="ffn",
    )(x, attn, u, u, gates, mod, wo, cw, cb, lng, lnb, wpw, wout, g2, w1, w2)


def _head_blocks(w, width):
    return w.reshape(w.shape[0], N_HEADS, width)


def _pad_heads(w3):
    k, h, w = w3.shape
    return jnp.pad(w3, ((0, 0), (0, 0), (0, HEAD_BLOCK - w))).reshape(k, h * HEAD_BLOCK)


def _swap_rope_halves(w3):
    nope = jnp.zeros_like(w3[..., :QK_NOPE_DIM])
    lo = w3[..., QK_NOPE_DIM:QK_NOPE_DIM + ROPE_HALF]
    hi = w3[..., QK_NOPE_DIM + ROPE_HALF:]
    return jnp.concatenate([nope, hi, lo], axis=-1)


def _rope_gain_tables(seq, gain, premul):
    inv_freq = ROPE_THETA ** (-jnp.arange(0, QK_ROPE_DIM, 2, dtype=F32) / QK_ROPE_DIM)
    ang = jnp.arange(seq, dtype=F32)[:, None] * inv_freq[None, :]
    cos, sin = jnp.cos(ang), jnp.sin(ang)
    pad = HEAD_BLOCK - QK_HEAD_DIM
    g_nope = gain[:QK_NOPE_DIM]
    g_lo = gain[QK_NOPE_DIM:QK_NOPE_DIM + ROPE_HALF]
    g_hi = gain[QK_NOPE_DIM + ROPE_HALF:]
    a = jnp.concatenate([jnp.broadcast_to(g_nope, (seq, QK_NOPE_DIM)), cos * g_lo, cos * g_hi,
                         jnp.zeros((seq, pad), F32)], axis=-1)
    b = jnp.concatenate([jnp.zeros((seq, QK_NOPE_DIM), F32), -sin * g_hi, sin * g_lo,
                         jnp.zeros((seq, pad), F32)], axis=-1)
    return a * premul, b * premul


def kernel(x, c, w_ada, b_ada, norm1_g, w_in, q_latent_g, w_uq, kv_latent_g, w_ukv, qk_norm_q_g,
           qk_norm_k_g, w_o_mla, conv_w, conv_b, conv_ln_g, conv_ln_b, w_pw_out, w_out, norm2_g,
           w_ff1, w_ff2):
    bsz, seq, d = x.shape
    depth = w_ada.shape[0]
    off_q = Q_LORA_RANK
    off_kv = off_q + KV_LORA_RANK
    off_kr = off_kv + QK_ROPE_DIM
    off_glu = off_kr + 2 * CONV_CH
    row = lambda v: v.reshape(1, -1)

    q_premul = (QK_HEAD_DIM ** -0.5) * math.log2(math.e)
    vone = jnp.zeros((N_HEADS, HEAD_BLOCK), F32)
    vone = vone.at[0::2, V_HEAD_DIM].set(1.0).at[1::2, 0].set(1.0).reshape(1, -1)

    for l in range(depth):
        wi = w_in[l]
        kr_cols = wi[:, off_kv:off_kr]
        kr_swapped = jnp.concatenate([kr_cols[:, ROPE_HALF:], kr_cols[:, :ROPE_HALF]], axis=-1)
        wa = jnp.concatenate(
            [wi[:, :off_kr], kr_swapped,
             jnp.zeros((d, 4 * LANES - off_kr - QK_ROPE_DIM), wi.dtype)], axis=-1).astype(BF16)
        wglu = wi[:, off_kr:off_glu].astype(BF16)
        wgate = wi[:, off_glu:].astype(BF16)

        uq3 = _head_blocks(w_uq[l], QK_HEAD_DIM)
        wuq = _pad_heads(uq3).astype(BF16)
        wuqs = _pad_heads(_swap_rope_halves(uq3)).astype(BF16)

        ukv3 = _head_blocks(w_ukv[l], QK_NOPE_DIM + V_HEAD_DIM)
        wuk = _pad_heads(ukv3[..., :QK_NOPE_DIM])
        v3 = ukv3[..., QK_NOPE_DIM:]
        zero_v = jnp.zeros_like(v3)
        even = (jnp.arange(N_HEADS) % 2 == 0)[None, :, None]
        wuv = jnp.where(even, jnp.concatenate([v3, zero_v], -1),
                        jnp.concatenate([zero_v, v3], -1)).reshape(KV_LORA_RANK, -1)
        wukv = jnp.concatenate([wuk, wuv], axis=-1).astype(BF16)

        aq, bq = _rope_gain_tables(seq, qk_norm_q_g[l], q_premul)
        ak, bk = _rope_gain_tables(seq, qk_norm_k_g[l], 1.0)

        mod = _adaln(c, w_ada[l], b_ada[l]).reshape(bsz, ADA_CHUNKS, d)
        q, k, v, u, gates = _proj(
            x, mod, row(norm1_g[l]), wa, wglu, wgate, row(q_latent_g[l]), wuq, wuqs,
            row(kv_latent_g[l]), wukv, aq, bq, ak, bk, vone)
        attn = _attn(q, k, v)
        x = _ffn(x, attn, u, gates, mod, w_o_mla[l].astype(BF16), conv_w[l], row(conv_b[l]),
                 row(conv_ln_g[l]), row(conv_ln_b[l]), w_pw_out[l].astype(BF16),
                 w_out[l].astype(BF16), row(norm2_g[l]), w_ff1[l].astype(BF16),
                 w_ff2[l].astype(BF16))
    return x
```

```python
import functools
import math

import jax
import jax.numpy as jnp
from jax import lax
from jax.experimental import pallas as pl
from jax.experimental.pallas import tpu as pltpu

F32 = jnp.float32
BF16 = jnp.bfloat16

CHUNK = 64
N_HEADS = 8
QK_NOPE_DIM = 64
QK_ROPE_DIM = 32
QK_HEAD_DIM = QK_NOPE_DIM + QK_ROPE_DIM
V_HEAD_DIM = 64
Q_LORA_RANK = 256
KV_LORA_RANK = 128
CONV_CH = 512
CONV_WIDTH = 31
ADA_CHUNKS = 6
ROPE_THETA = 10000.0
EPS = 1e-6

LANES = 128
SUBLANES = 8
VMEM_LIMIT_BYTES = 56 * 1024 * 1024

HEAD_BLOCK = LANES
HALO = 32
ROPE_HALF = QK_ROPE_DIM // 2

PROJ_TILE = 512
FFN_TILE = 512
ATTN_QBLOCK = 256
CONV_ROWS = 32
CONV_GROUP = 128


def _rms_scale(x, width):
    return lax.rsqrt(jnp.sum(x * x, axis=-1, keepdims=True) * (1.0 / width) + EPS)


def _adaln_kernel(c_ref, w_ref, b_ref, o_ref):
    c = c_ref[...]
    act = (c * jax.nn.sigmoid(c)).astype(BF16)
    o_ref[...] = jnp.dot(act, w_ref[...].astype(BF16), preferred_element_type=F32) + b_ref[...]


def _adaln(c, w, b):
    bsz, d = c.shape
    n = w.shape[1]
    tn = 1024
    return pl.pallas_call(
        _adaln_kernel,
        out_shape=jax.ShapeDtypeStruct((bsz, n), F32),
        grid=(n // tn,),
        in_specs=[
            pl.BlockSpec((bsz, d), lambda j: (0, 0)),
            pl.BlockSpec((d, tn), lambda j: (0, j)),
            pl.BlockSpec((1, tn), lambda j: (0, j)),
        ],
        out_specs=pl.BlockSpec((bsz, tn), lambda j: (0, j)),
        compiler_params=pltpu.CompilerParams(
            dimension_semantics=("arbitrary",), vmem_limit_bytes=VMEM_LIMIT_BYTES),
        name="adaln",
    )(c, w, b.reshape(1, n))


def _proj_kernel(x_ref, mod_ref, g1_ref, wa_ref, wglu_ref, wgate_ref, gq_ref, wuq_ref, wuqs_ref,
                 gkv_ref, wukv_ref, aq_ref, bq_ref, ak_ref, bk_ref, vone_ref,
                 q_ref, k_ref, v_ref, u_ref, gate_ref):
    d_qk = N_HEADS * HEAD_BLOCK
    x = x_ref[0]
    scale_row = g1_ref[...] * (1.0 + mod_ref[0, 1:2, :])
    shift_row = mod_ref[0, 0:1, :]
    hb = (x * _rms_scale(x, x.shape[-1]) * scale_row + shift_row).astype(BF16)

    z_a = jnp.dot(hb, wa_ref[...], preferred_element_type=F32)

    zq = z_a[:, :Q_LORA_RANK]
    qn = (zq * _rms_scale(zq, Q_LORA_RANK) * gq_ref[...]).astype(BF16)
    q_pre = jnp.dot(qn, wuq_ref[...], preferred_element_type=F32)
    q_swp = jnp.dot(qn, wuqs_ref[...], preferred_element_type=F32)
    aq = aq_ref[...]
    bq = bq_ref[...]
    for h in range(N_HEADS):
        sl = slice(h * HEAD_BLOCK, (h + 1) * HEAD_BLOCK)
        blk = q_pre[:, sl]
        qh = _rms_scale(blk, QK_HEAD_DIM) * (blk * aq + q_swp[:, sl] * bq)
        q_ref[0, :, sl] = qh.astype(BF16)

    zkv = z_a[:, Q_LORA_RANK:Q_LORA_RANK + KV_LORA_RANK]
    kvn = (zkv * _rms_scale(zkv, KV_LORA_RANK) * gkv_ref[...]).astype(BF16)
    kv_up = jnp.dot(kvn, wukv_ref[...], preferred_element_type=F32)

    kr_blk = z_a[:, Q_LORA_RANK + KV_LORA_RANK:]
    lane = lax.broadcasted_iota(jnp.int32, (1, LANES), 1)
    kr_only = jnp.where(lane < QK_ROPE_DIM, kr_blk, 0.0)
    ss_kr = jnp.sum(kr_only * kr_only, axis=-1, keepdims=True)
    ak = ak_ref[...]
    bk = bk_ref[...]
    kr_rot = (pltpu.roll(kr_blk, QK_NOPE_DIM, 1) * ak
              + pltpu.roll(kr_blk, QK_NOPE_DIM - QK_ROPE_DIM, 1) * bk)
    for h in range(N_HEADS):
        sl = slice(h * HEAD_BLOCK, (h + 1) * HEAD_BLOCK)
        blk = kv_up[:, sl]
        ss = jnp.sum(blk * blk, axis=-1, keepdims=True) + ss_kr
        kh = lax.rsqrt(ss * (1.0 / QK_HEAD_DIM) + EPS) * (blk * ak + kr_rot)
        k_ref[0, :, sl] = kh.astype(BF16)

    v_ref[0] = (kv_up[:, d_qk:] + vone_ref[...]).astype(BF16)

    zg = jnp.dot(hb, wglu_ref[...], preferred_element_type=F32)
    u_ref[0] = zg[:, :CONV_CH] * jax.nn.sigmoid(zg[:, CONV_CH:])

    gate_ref[0] = jax.nn.sigmoid(
        jnp.dot(hb, wgate_ref[...], preferred_element_type=F32)).astype(BF16)


def _const_spec(shape):
    nd = len(shape)
    return pl.BlockSpec(shape, lambda *_: (0,) * nd, pipeline_mode=pl.Buffered(1))


def _proj(x, mod, g1, wa, wglu, wgate, gq, wuq, wuqs, gkv, wukv, aq, bq, ak, bk, vone):
    bsz, seq, d = x.shape
    t = PROJ_TILE
    d_qk = N_HEADS * HEAD_BLOCK
    tok = lambda w: pl.BlockSpec((1, t, w), lambda b, s: (b, s, 0))
    pos = lambda: pl.BlockSpec((t, LANES), lambda b, s: (s, 0))
    out_shapes = (
        jax.ShapeDtypeStruct((bsz, seq, d_qk), BF16),
        jax.ShapeDtypeStruct((bsz, seq, d_qk), BF16),
        jax.ShapeDtypeStruct((bsz, seq, d_qk), BF16),
        jax.ShapeDtypeStruct((bsz, seq, CONV_CH), F32),
        jax.ShapeDtypeStruct((bsz, seq, 2 * d), BF16),
    )
    return pl.pallas_call(
        _proj_kernel,
        out_shape=out_shapes,
        grid=(bsz, seq // t),
        in_specs=[
            tok(d),
            pl.BlockSpec((1, ADA_CHUNKS, d), lambda b, s: (b, 0, 0)),
            _const_spec(g1.shape), _const_spec(wa.shape), _const_spec(wglu.shape),
            _const_spec(wgate.shape), _const_spec(gq.shape), _const_spec(wuq.shape),
            _const_spec(wuqs.shape), _const_spec(gkv.shape), _const_spec(wukv.shape),
            pos(), pos(), pos(), pos(), _const_spec(vone.shape),
        ],
        out_specs=(tok(d_qk), tok(d_qk), tok(d_qk), tok(CONV_CH), tok(2 * d)),
        compiler_params=pltpu.CompilerParams(
            dimension_semantics=("arbitrary", "arbitrary"), vmem_limit_bytes=VMEM_LIMIT_BYTES),
        name="proj",
    )(x, mod, g1, wa, wglu, wgate, gq, wuq, wuqs, gkv, wukv, aq, bq, ak, bk, vone)


def _attn_kernel(q_ref, k_ref, v_ref, o_ref):
    seq = q_ref.shape[1]
    qb = ATTN_QBLOCK
    neg = jnp.finfo(F32).min
    row_chunk = lax.broadcasted_iota(jnp.int32, (qb, qb), 0) // CHUNK
    col_chunk = lax.broadcasted_iota(jnp.int32, (qb, qb), 1) // CHUNK
    diag_mask = col_chunk <= row_chunk
    lane = lax.broadcasted_iota(jnp.int32, (qb, LANES), 1)
    contract_last = (((1,), (1,)), ((), ()))
    for j in range(seq // qb):
        q0 = j * qb
        q1 = q0 + qb
        outs = []
        for hh in range(2):
            sl = slice(hh * HEAD_BLOCK, (hh + 1) * HEAD_BLOCK)
            qt = q_ref[0, q0:q1, sl]
            s_d = lax.dot_general(qt, k_ref[0, q0:q1, sl], contract_last, preferred_element_type=F32)
            s_d = jnp.where(diag_mask, s_d, neg)
            m = jnp.max(s_d, axis=-1, keepdims=True)
            if j > 0:
                s_o = lax.dot_general(qt, k_ref[0, 0:q0, sl], contract_last, preferred_element_type=F32)
                m = jnp.maximum(m, jnp.max(s_o, axis=-1, keepdims=True))
            p_d = jnp.exp2(s_d - m).astype(BF16)
            o = jnp.dot(p_d, v_ref[0, q0:q1, sl], preferred_element_type=F32)
            if j > 0:
                p_o = jnp.exp2(s_o - m).astype(BF16)
                o = o + jnp.dot(p_o, v_ref[0, 0:q0, sl], preferred_element_type=F32)
            denom = o[:, V_HEAD_DIM:V_HEAD_DIM + 1] if hh == 0 else o[:, 0:1]
            outs.append(o / denom)
        o_ref[0, q0:q1, :] = jnp.where(lane < V_HEAD_DIM, outs[0], outs[1]).astype(BF16)


def _attn(q, k, v):
    bsz, seq, d_qk = q.shape
    pair = 2 * HEAD_BLOCK
    spec = pl.BlockSpec((1, seq, pair), lambda b, p: (b, 0, p))
    return pl.pallas_call(
        _attn_kernel,
        out_shape=jax.ShapeDtypeStruct((bsz, seq, N_HEADS * V_HEAD_DIM), BF16),
        grid=(bsz, d_qk // pair),
        in_specs=[spec, spec, spec],
        out_specs=pl.BlockSpec((1, seq, 2 * V_HEAD_DIM), lambda b, p: (b, 0, p)),
        compiler_params=pltpu.CompilerParams(
            dimension_semantics=("arbitrary", "arbitrary"), vmem_limit_bytes=VMEM_LIMIT_BYTES),
        name="attn",
    )(q, k, v)


def _conv_group(g, u_ref, halo, out_ref, ubuf, wbc, cb_ref, lng_ref, lnb_ref):
    g0 = pl.multiple_of(g * CONV_GROUP, CONV_GROUP)
    prev0 = pl.multiple_of(jnp.maximum(g0 - HALO, 0), HALO)
    ext = CONV_GROUP + HALO
    ubuf[0, 0:HALO, :] = jnp.where(g == 0, halo, u_ref[pl.ds(prev0, HALO), :])
    ubuf[0, HALO:, :] = u_ref[pl.ds(g0, CONV_GROUP), :]
    for p in range(1, SUBLANES):
        ubuf[p, 0:ext - SUBLANES, :] = ubuf[0, p:ext - SUBLANES + p, :]

    tap0 = HALO - (CONV_WIDTH - 1)
    blocks = CONV_ROWS // SUBLANES
    for c in range(CONV_GROUP // CONV_ROWS):
        r0 = c * CONV_ROWS
        acc = [jnp.broadcast_to(cb_ref[...], (SUBLANES, CONV_CH))] * blocks
        for tap in range(CONV_WIDTH):
            blk, phase = divmod(tap0 + tap, SUBLANES)
            w = wbc[tap]
            for rb in range(blocks):
                row = r0 + (blk + rb) * SUBLANES
                acc[rb] = acc[rb] + ubuf[phase, row:row + SUBLANES, :] * w
        conv = jnp.concatenate(acc, axis=0)
        cen = conv - jnp.mean(conv, axis=-1, keepdims=True)
        var = jnp.mean(cen * cen, axis=-1, keepdims=True)
        y = cen * lax.rsqrt(var + EPS) * lng_ref[...] + lnb_ref[...]
        out_ref[pl.ds(g0 + r0, CONV_ROWS), :] = (y * jax.nn.sigmoid(y)).astype(BF16)


def _ffn_kernel(x_ref, attn_ref, gate_ref, mod_ref, ufirst_ref, unext_ref, halo_ref,
                wo_ref, cw_ref, cb_ref, lng_ref, lnb_ref, wpw_ref, wout_ref, g2_ref, w1_ref, w2_ref,
                o_ref, ubuf, wbc, cbuf, h2_ref, f_ref, *, tiles_per_seq):
    t, d = x_ref.shape
    i = pl.program_id(0)
    slot = i % 2
    groups = t // CONV_GROUP
    conv_args = (ubuf, wbc, cb_ref, lng_ref, lnb_ref)

    @pl.when(i == 0)
    def _():
        for tap in range(CONV_WIDTH):
            wbc[tap] = jnp.broadcast_to(cw_ref[tap:tap + 1, :], (SUBLANES, CONV_CH))
        zero_halo = jnp.zeros((HALO, CONV_CH), F32)

        def first_tile(g, carry):
            _conv_group(g, ufirst_ref, zero_halo, cbuf.at[0], *conv_args)
            return carry

        lax.fori_loop(0, groups, first_tile, 0)

    y_a = jnp.dot(attn_ref[...], wo_ref[...], preferred_element_type=F32)
    y_b = jnp.dot(cbuf[slot], wpw_ref[...], preferred_element_type=F32)
    g_a = gate_ref[:, :d].astype(F32)
    g_b = gate_ref[:, d:].astype(F32)
    mix = (g_a * y_a + g_b * y_b).astype(BF16)
    mixed = jnp.dot(mix, wout_ref[...], preferred_element_type=F32)

    x1 = x_ref[...] + mod_ref[0, 2:3, :] * mixed
    o_ref[...] = x1
    scale_row = g2_ref[...] * (1.0 + mod_ref[0, 4:5, :])
    h2_ref[...] = (x1 * _rms_scale(x1, d) * scale_row + mod_ref[0, 3:4, :]).astype(BF16)
    f_ref[...] = jnp.zeros_like(f_ref)

    next_starts_seq = (i + 1) % tiles_per_seq == 0
    halo_next = jnp.where(next_starts_seq, 0.0, halo_ref[...])

    def body(g, carry):
        _conv_group(g, unext_ref, halo_next, cbuf.at[1 - slot], *conv_args)
        hid = jnp.maximum(jnp.dot(h2_ref[...], w1_ref[g], preferred_element_type=F32), 0.0)
        f_ref[...] += jnp.dot((hid * hid).astype(BF16), w2_ref[g], preferred_element_type=F32)
        return carry

    lax.fori_loop(0, groups, body, 0)
    o_ref[...] = o_ref[...] + mod_ref[0, 5:6, :] * f_ref[...]


def _ffn(x, attn, u, gates, mod, wo, cw, cb, lng, lnb, wpw, wout, g2, w1, w2):
    bsz, seq, d = x.shape
    t = FFN_TILE
    groups = t // CONV_GROUP
    tiles_per_seq = seq // t
    n = bsz * tiles_per_seq
    rows = bsz * seq
    flat = lambda a: a.reshape(rows, a.shape[-1])
    w1c = w1.reshape(d, groups, -1).transpose(1, 0, 2)
    w2c = w2.reshape(groups, -1, d)
    tok = lambda w: pl.BlockSpec((t, w), lambda i: (i, 0))
    next_tile = lambda i: jnp.minimum(i + 1, n - 1)
    kern = functools.partial(_ffn_kernel, tiles_per_seq=tiles_per_seq)
    out = pl.pallas_call(
        kern,
        out_shape=jax.ShapeDtypeStruct((rows, d), x.dtype),
        grid=(n,),
        in_specs=[
            tok(d), tok(attn.shape[-1]), tok(2 * d),
            pl.BlockSpec((1, ADA_CHUNKS, d), lambda i: (i // tiles_per_seq, 0, 0)),
            pl.BlockSpec((t, CONV_CH), lambda i: (0, 0), pipeline_mode=pl.Buffered(1)),
            pl.BlockSpec((t, CONV_CH), lambda i: (next_tile(i), 0)),
            pl.BlockSpec((HALO, CONV_CH), lambda i: (next_tile(i) * (t // HALO) - 1, 0)),
            _const_spec(wo.shape), _const_spec(cw.shape), _const_spec(cb.shape),
            _const_spec(lng.shape), _const_spec(lnb.shape), _const_spec(wpw.shape),
            _const_spec(wout.shape), _const_spec(g2.shape), _const_spec(w1c.shape),
            _const_spec(w2c.shape),
        ],
        out_specs=tok(d),
        scratch_shapes=[
            pltpu.VMEM((SUBLANES, CONV_GROUP + HALO, CONV_CH), F32),
            pltpu.VMEM((CONV_WIDTH, SUBLANES, CONV_CH), F32),
            pltpu.VMEM((2, t, CONV_CH), BF16),
            pltpu.VMEM((t, d), BF16),
            pltpu.VMEM((t, d), F32),
        ],
        compiler_params=pltpu.CompilerParams(
            dimension_semantics=("arbitrary",), vmem_limit_bytes=VMEM_LIMIT_BYTES),
        name="ffn",
    )(flat(x), flat(attn), flat(gates), mod, flat(u), flat(u), flat(u),
      wo, cw, cb, lng, lnb, wpw, wout, g2, w1c, w2c)
    return out.reshape(bsz, seq, d)


def _head_blocks(w, width):
    return w.reshape(w.shape[0], N_HEADS, width)


def _pad_heads(w3):
    k, h, w = w3.shape
    return jnp.pad(w3, ((0, 0), (0, 0), (0, HEAD_BLOCK - w))).reshape(k, h * HEAD_BLOCK)


def _swap_rope_halves(w3):
    nope = jnp.zeros_like(w3[..., :QK_NOPE_DIM])
    lo = w3[..., QK_NOPE_DIM:QK_NOPE_DIM + ROPE_HALF]
    hi = w3[..., QK_NOPE_DIM + ROPE_HALF:]
    return jnp.concatenate([nope, hi, lo], axis=-1)


def _rope_gain_tables(seq, gain, premul):
    inv_freq = ROPE_THETA ** (-jnp.arange(0, QK_ROPE_DIM, 2, dtype=F32) / QK_ROPE_DIM)
    ang = jnp.arange(seq, dtype=F32)[:, None] * inv_freq[None, :]
    cos, sin = jnp.cos(ang), jnp.sin(ang)
    pad = HEAD_BLOCK - QK_HEAD_DIM
    g_nope = gain[:QK_NOPE_DIM]
    g_lo = gain[QK_NOPE_DIM:QK_NOPE_DIM + ROPE_HALF]
    g_hi = gain[QK_NOPE_DIM + ROPE_HALF:]
    a = jnp.concatenate([jnp.broadcast_to(g_nope, (seq, QK_NOPE_DIM)), cos * g_lo, cos * g_hi,
                         jnp.zeros((seq, pad), F32)], axis=-1)
    b = jnp.concatenate([jnp.zeros((seq, QK_NOPE_DIM), F32), -sin * g_hi, sin * g_lo,
                         jnp.zeros((seq, pad), F32)], axis=-1)
    return a * premul, b * premul


def kernel(x, c, w_ada, b_ada, norm1_g, w_in, q_latent_g, w_uq, kv_latent_g, w_ukv, qk_norm_q_g,
           qk_norm_k_g, w_o_mla, conv_w, conv_b, conv_ln_g, conv_ln_b, w_pw_out, w_out, norm2_g,
           w_ff1, w_ff2):
    bsz, seq, d = x.shape
    depth = w_ada.shape[0]
    off_q = Q_LORA_RANK
    off_kv = off_q + KV_LORA_RANK
    off_kr = off_kv + QK_ROPE_DIM
    off_glu = off_kr + 2 * CONV_CH
    row = lambda v: v.reshape(1, -1)

    q_premul = (QK_HEAD_DIM ** -0.5) * math.log2(math.e)
    vone = jnp.zeros((N_HEADS, HEAD_BLOCK), F32)
    vone = vone.at[0::2, V_HEAD_DIM].set(1.0).at[1::2, 0].set(1.0).reshape(1, -1)

    for l in range(depth):
        wi = w_in[l]
        kr_cols = wi[:, off_kv:off_kr]
        kr_swapped = jnp.concatenate([kr_cols[:, ROPE_HALF:], kr_cols[:, :ROPE_HALF]], axis=-1)
        wa = jnp.concatenate(
            [wi[:, :off_kr], kr_swapped,
             jnp.zeros((d, 4 * LANES - off_kr - QK_ROPE_DIM), wi.dtype)], axis=-1).astype(BF16)
        wglu = wi[:, off_kr:off_glu].astype(BF16)
        wgate = wi[:, off_glu:].astype(BF16)

        uq3 = _head_blocks(w_uq[l], QK_HEAD_DIM)
        wuq = _pad_heads(uq3).astype(BF16)
        wuqs = _pad_heads(_swap_rope_halves(uq3)).astype(BF16)

        ukv3 = _head_blocks(w_ukv[l], QK_NOPE_DIM + V_HEAD_DIM)
        wuk = _pad_heads(ukv3[..., :QK_NOPE_DIM])
        v3 = ukv3[..., QK_NOPE_DIM:]
        zero_v = jnp.zeros_like(v3)
        even = (jnp.arange(N_HEADS) % 2 == 0)[None, :, None]
        wuv = jnp.where(even, jnp.concatenate([v3, zero_v], -1),
                        jnp.concatenate([zero_v, v3], -1)).reshape(KV_LORA_RANK, -1)
        wukv = jnp.concatenate([wuk, wuv], axis=-1).astype(BF16)

        aq, bq = _rope_gain_tables(seq, qk_norm_q_g[l], q_premul)
        ak, bk = _rope_gain_tables(seq, qk_norm_k_g[l], 1.0)

        mod = _adaln(c, w_ada[l], b_ada[l]).reshape(bsz, ADA_CHUNKS, d)
        q, k, v, u, gates = _proj(
            x, mod, row(norm1_g[l]), wa, wglu, wgate, row(q_latent_g[l]), wuq, wuqs,
            row(kv_latent_g[l]), wukv, aq, bq, ak, bk, vone)
        attn = _attn(q, k, v)
        x = _ffn(x, attn, u, gates, mod, w_o_mla[l].astype(BF16), conv_w[l], row(conv_b[l]),
                 row(conv_ln_g[l]), row(conv_ln_b[l]), w_pw_out[l].astype(BF16),
                 w_out[l].astype(BF16), row(norm2_g[l]), w_ff1[l].astype(BF16),
                 w_ff2[l].astype(BF16))
    return x
```

```python
import functools
import math

import jax
import jax.numpy as jnp
from jax import lax
from jax.experimental import pallas as pl
from jax.experimental.pallas import tpu as pltpu

F32 = jnp.float32
BF16 = jnp.bfloat16

CHUNK = 64
N_HEADS = 8
QK_NOPE_DIM = 64
QK_ROPE_DIM = 32
QK_HEAD_DIM = QK_NOPE_DIM + QK_ROPE_DIM
V_HEAD_DIM = 64
Q_LORA_RANK = 256
KV_LORA_RANK = 128
CONV_CH = 512
CONV_WIDTH = 31
ADA_CHUNKS = 6
ROPE_THETA = 10000.0
EPS = 1e-6

LANES = 128
SUBLANES = 8
VMEM_LIMIT_BYTES = 56 * 1024 * 1024

HEAD_BLOCK = LANES
VT_ROWS = 80
HALO = 32
ROPE_HALF = QK_ROPE_DIM // 2

PROJ_TILE = 512
FFN_TILE = 512
ATTN_QBLOCK = 256
ATTN_KCHUNK = 512
CONV_ROWS = 32
CONV_GROUP = 128


def _rms_scale(x, width):
    return lax.rsqrt(jnp.sum(x * x, axis=-1, keepdims=True) * (1.0 / width) + EPS)


def _adaln_kernel(c_ref, w_ref, b_ref, o_ref):
    c = c_ref[...]
    act = (c * jax.nn.sigmoid(c)).astype(BF16)
    o_ref[...] = jnp.dot(act, w_ref[...].astype(BF16), preferred_element_type=F32) + b_ref[...]


def _adaln(c, w, b):
    bsz, d = c.shape
    n = w.shape[1]
    tn = 1024
    return pl.pallas_call(
        _adaln_kernel,
        out_shape=jax.ShapeDtypeStruct((bsz, n), F32),
        grid=(n // tn,),
        in_specs=[
            pl.BlockSpec((bsz, d), lambda j: (0, 0)),
            pl.BlockSpec((d, tn), lambda j: (0, j)),
            pl.BlockSpec((1, tn), lambda j: (0, j)),
        ],
        out_specs=pl.BlockSpec((bsz, tn), lambda j: (0, j)),
        compiler_params=pltpu.CompilerParams(
            dimension_semantics=("arbitrary",), vmem_limit_bytes=VMEM_LIMIT_BYTES),
        name="adaln",
    )(c, w, b.reshape(1, n))


def _proj_kernel(x_ref, mod_ref, g1_ref, wa_ref, wglu_ref, wgate_ref, gq_ref, wuq_ref, wuqs_ref,
                 gkv_ref, wuk_ref, wuvt_ref, aq_ref, bq_ref, ak_ref, bk_ref,
                 q_ref, k_ref, vt_ref, u_ref, gate_ref):
    x = x_ref[0]
    scale_row = g1_ref[...] * (1.0 + mod_ref[0, 1:2, :])
    shift_row = mod_ref[0, 0:1, :]
    hb = (x * _rms_scale(x, x.shape[-1]) * scale_row + shift_row).astype(BF16)

    z_a = jnp.dot(hb, wa_ref[...], preferred_element_type=F32)

    zq = z_a[:, :Q_LORA_RANK]
    qn = (zq * _rms_scale(zq, Q_LORA_RANK) * gq_ref[...]).astype(BF16)
    q_pre = jnp.dot(qn, wuq_ref[...], preferred_element_type=F32)
    q_swp = jnp.dot(qn, wuqs_ref[...], preferred_element_type=F32)
    aq = aq_ref[...]
    bq = bq_ref[...]
    for h in range(N_HEADS):
        sl = slice(h * HEAD_BLOCK, (h + 1) * HEAD_BLOCK)
        blk = q_pre[:, sl]
        qh = _rms_scale(blk, QK_HEAD_DIM) * (blk * aq + q_swp[:, sl] * bq)
        q_ref[0, :, sl] = qh.astype(BF16)

    zkv = z_a[:, Q_LORA_RANK:Q_LORA_RANK + KV_LORA_RANK]
    kvn = (zkv * _rms_scale(zkv, KV_LORA_RANK) * gkv_ref[...]).astype(BF16)
    k_up = jnp.dot(kvn, wuk_ref[...], preferred_element_type=F32)

    kr_blk = z_a[:, Q_LORA_RANK + KV_LORA_RANK:]
    lane = lax.broadcasted_iota(jnp.int32, (1, LANES), 1)
    kr_only = jnp.where(lane < QK_ROPE_DIM, kr_blk, 0.0)
    ss_kr = jnp.sum(kr_only * kr_only, axis=-1, keepdims=True)
    ak = ak_ref[...]
    bk = bk_ref[...]
    kr_rot = (pltpu.roll(kr_blk, QK_NOPE_DIM, 1) * ak
              + pltpu.roll(kr_blk, QK_NOPE_DIM - QK_ROPE_DIM, 1) * bk)
    for h in range(N_HEADS):
        sl = slice(h * HEAD_BLOCK, (h + 1) * HEAD_BLOCK)
        blk = k_up[:, sl]
        ss = jnp.sum(blk * blk, axis=-1, keepdims=True) + ss_kr
        kh = lax.rsqrt(ss * (1.0 / QK_HEAD_DIM) + EPS) * (blk * ak + kr_rot)
        k_ref[0, :, sl] = kh.astype(BF16)

    vt = lax.dot_general(wuvt_ref[...], kvn, (((1,), (1,)), ((), ())), preferred_element_type=F32)
    row = lax.broadcasted_iota(jnp.int32, vt.shape, 0)
    vt_ref[0] = jnp.where(row % VT_ROWS == V_HEAD_DIM, 1.0, vt).astype(BF16)

    zg = jnp.dot(hb, wglu_ref[...], preferred_element_type=F32)
    u_ref[0] = zg[:, :CONV_CH] * jax.nn.sigmoid(zg[:, CONV_CH:])

    gate_ref[0] = jax.nn.sigmoid(
        jnp.dot(hb, wgate_ref[...], preferred_element_type=F32)).astype(BF16)


def _const_spec(shape):
    nd = len(shape)
    return pl.BlockSpec(shape, lambda *_: (0,) * nd, pipeline_mode=pl.Buffered(1))


def _proj(x, mod, g1, wa, wglu, wgate, gq, wuq, wuqs, gkv, wuk, wuvt, aq, bq, ak, bk):
    bsz, seq, d = x.shape
    t = PROJ_TILE
    d_qk = N_HEADS * HEAD_BLOCK
    vt_rows = N_HEADS * VT_ROWS
    tok = lambda w: pl.BlockSpec((1, t, w), lambda b, s: (b, s, 0))
    pos = lambda: pl.BlockSpec((t, LANES), lambda b, s: (s, 0))
    out_shapes = (
        jax.ShapeDtypeStruct((bsz, seq, d_qk), BF16),
        jax.ShapeDtypeStruct((bsz, seq, d_qk), BF16),
        jax.ShapeDtypeStruct((bsz, vt_rows, seq), BF16),
        jax.ShapeDtypeStruct((bsz, seq, CONV_CH), F32),
        jax.ShapeDtypeStruct((bsz, seq, 2 * d), BF16),
    )
    return pl.pallas_call(
        _proj_kernel,
        out_shape=out_shapes,
        grid=(bsz, seq // t),
        in_specs=[
            tok(d),
            pl.BlockSpec((1, ADA_CHUNKS, d), lambda b, s: (b, 0, 0)),
            _const_spec(g1.shape), _const_spec(wa.shape), _const_spec(wglu.shape),
            _const_spec(wgate.shape), _const_spec(gq.shape), _const_spec(wuq.shape),
            _const_spec(wuqs.shape), _const_spec(gkv.shape), _const_spec(wuk.shape),
            _const_spec(wuvt.shape), pos(), pos(), pos(), pos(),
        ],
        out_specs=(tok(d_qk), tok(d_qk),
                   pl.BlockSpec((1, vt_rows, t), lambda b, s: (b, 0, s)),
                   tok(CONV_CH), tok(2 * d)),
        compiler_params=pltpu.CompilerParams(
            dimension_semantics=("arbitrary", "arbitrary"), vmem_limit_bytes=VMEM_LIMIT_BYTES),
        name="proj",
    )(x, mod, g1, wa, wglu, wgate, gq, wuq, wuqs, gkv, wuk, wuvt, aq, bq, ak, bk)


def _attn_kernel(q_ref, k_ref, vt_ref, o_ref):
    seq = q_ref.shape[1]
    qb = ATTN_QBLOCK
    neg = jnp.finfo(F32).min
    key_chunk = lax.broadcasted_iota(jnp.int32, (qb, qb), 0) // CHUNK
    query_chunk = lax.broadcasted_iota(jnp.int32, (qb, qb), 1) // CHUNK
    diag_mask = key_chunk <= query_chunk
    contract_last = (((1,), (1,)), ((), ()))
    blocks = [(j, hh) for j in range(seq // qb) for hh in range(2)]

    def key_chunks(j):
        q0 = j * qb
        edges = list(range(0, q0, ATTN_KCHUNK)) + [q0]
        return [(a, b, False) for a, b in zip(edges[:-1], edges[1:])] + [(q0, q0 + qb, True)]

    def score_steps(j, hh, out):
        sl = slice(hh * HEAD_BLOCK, (hh + 1) * HEAD_BLOCK)
        qt = q_ref[0, j * qb:(j + 1) * qb, sl]
        m = None
        for a, b, diag in key_chunks(j):
            s = lax.dot_general(k_ref[0, a:b, sl], qt, contract_last, preferred_element_type=F32)
            if diag:
                s = jnp.where(diag_mask, s, neg)
            cm = jnp.max(s, axis=0, keepdims=True)
            m = cm if m is None else jnp.maximum(m, cm)
            out.append(s)
            yield
        out.append(m)

    def value_steps(j, hh, sc, out):
        rows = slice(hh * VT_ROWS, (hh + 1) * VT_ROWS)
        m = sc[-1]
        ot = None
        for (a, b, _), s in zip(key_chunks(j), sc[:-1]):
            part = jnp.dot(vt_ref[0, rows, a:b], jnp.exp2(s - m).astype(BF16),
                           preferred_element_type=F32)
            ot = part if ot is None else ot + part
            yield
        out.append(ot[:V_HEAD_DIM] / ot[V_HEAD_DIM:V_HEAD_DIM + 1])

    halves = []
    pending = []
    for _ in score_steps(*blocks[0], pending):
        pass
    for n, (j, hh) in enumerate(blocks):
        current, pending = pending, []
        nxt = score_steps(*blocks[n + 1], pending) if n + 1 < len(blocks) else iter(())
        cur = value_steps(j, hh, current, halves)
        live = [nxt, cur]
        while live:
            for it in list(live):
                if next(it, StopIteration) is StopIteration:
                    live.remove(it)
        if hh == 1:
            o_ref[0, j * qb:(j + 1) * qb, :] = jnp.concatenate(halves, axis=0).T.astype(BF16)
            halves = []


def _attn(q, k, vt):
    bsz, seq, d_qk = q.shape
    pair = 2 * HEAD_BLOCK
    spec = pl.BlockSpec((1, seq, pair), lambda b, p: (b, 0, p))
    return pl.pallas_call(
        _attn_kernel,
        out_shape=jax.ShapeDtypeStruct((bsz, seq, N_HEADS * V_HEAD_DIM), BF16),
        grid=(bsz, d_qk // pair),
        in_specs=[spec, spec, pl.BlockSpec((1, 2 * VT_ROWS, seq), lambda b, p: (b, p, 0))],
        out_specs=pl.BlockSpec((1, seq, 2 * V_HEAD_DIM), lambda b, p: (b, 0, p)),
        compiler_params=pltpu.CompilerParams(
            dimension_semantics=("arbitrary", "arbitrary"), vmem_limit_bytes=VMEM_LIMIT_BYTES),
        name="attn",
    )(q, k, vt)


def _conv_group(g, u_ref, halo, out_ref, ubuf, wbc, cb_ref, lng_ref, lnb_ref):
    g0 = pl.multiple_of(g * CONV_GROUP, CONV_GROUP)
    prev0 = pl.multiple_of(jnp.maximum(g0 - HALO, 0), HALO)
    ext = CONV_GROUP + HALO
    ubuf[0, 0:HALO, :] = jnp.where(g == 0, halo, u_ref[pl.ds(prev0, HALO), :])
    ubuf[0, HALO:, :] = u_ref[pl.ds(g0, CONV_GROUP), :]
    for p in range(1, SUBLANES):
        ubuf[p, 0:ext - SUBLANES, :] = ubuf[0, p:ext - SUBLANES + p, :]

    tap0 = HALO - (CONV_WIDTH - 1)
    blocks = CONV_ROWS // SUBLANES
    for c in range(CONV_GROUP // CONV_ROWS):
        r0 = c * CONV_ROWS
        acc = [jnp.broadcast_to(cb_ref[...], (SUBLANES, CONV_CH))] * blocks
        for tap in range(CONV_WIDTH):
            blk, phase = divmod(tap0 + tap, SUBLANES)
            w = wbc[tap]
            for rb in range(blocks):
                row = r0 + (blk + rb) * SUBLANES
                acc[rb] = acc[rb] + ubuf[phase, row:row + SUBLANES, :] * w
        conv = jnp.concatenate(acc, axis=0)
        cen = conv - jnp.mean(conv, axis=-1, keepdims=True)
        var = jnp.mean(cen * cen, axis=-1, keepdims=True)
        y = cen * lax.rsqrt(var + EPS) * lng_ref[...] + lnb_ref[...]
        out_ref[pl.ds(g0 + r0, CONV_ROWS), :] = (y * jax.nn.sigmoid(y)).astype(BF16)


def _ffn_kernel(x_ref, attn_ref, gate_ref, mod_ref, ufirst_ref, unext_ref, halo_ref,
                wo_ref, cw_ref, cb_ref, lng_ref, lnb_ref, wpw_ref, wout_ref, g2_ref, w1_ref, w2_ref,
                o_ref, ubuf, wbc, cbuf, h2_ref, f_ref, *, tiles_per_seq):
    t, d = x_ref.shape
    i = pl.program_id(0)
    slot = i % 2
    groups = t // CONV_GROUP
    conv_args = (ubuf, wbc, cb_ref, lng_ref, lnb_ref)

    @pl.when(i == 0)
    def _():
        for tap in range(CONV_WIDTH):
            wbc[tap] = jnp.broadcast_to(cw_ref[tap:tap + 1, :], (SUBLANES, CONV_CH))
        zero_halo = jnp.zeros((HALO, CONV_CH), F32)

        def first_tile(g, carry):
            _conv_group(g, ufirst_ref, zero_halo, cbuf.at[0], *conv_args)
            return carry

        lax.fori_loop(0, groups, first_tile, 0)

    y_a = jnp.dot(attn_ref[...], wo_ref[...], preferred_element_type=F32)
    y_b = jnp.dot(cbuf[slot], wpw_ref[...], preferred_element_type=F32)
    g_a = gate_ref[:, :d].astype(F32)
    g_b = gate_ref[:, d:].astype(F32)
    mix = (g_a * y_a + g_b * y_b).astype(BF16)
    mixed = jnp.dot(mix, wout_ref[...], preferred_element_type=F32)

    x1 = x_ref[...] + mod_ref[0, 2:3, :] * mixed
    o_ref[...] = x1
    scale_row = g2_ref[...] * (1.0 + mod_ref[0, 4:5, :])
    h2_ref[...] = (x1 * _rms_scale(x1, d) * scale_row + mod_ref[0, 3:4, :]).astype(BF16)
    f_ref[...] = jnp.zeros_like(f_ref)

    next_starts_seq = (i + 1) % tiles_per_seq == 0
    halo_next = jnp.where(next_starts_seq, 0.0, halo_ref[...])

    def body(g, carry):
        _conv_group(g, unext_ref, halo_next, cbuf.at[1 - slot], *conv_args)
        hid = jnp.maximum(jnp.dot(h2_ref[...], w1_ref[g], preferred_element_type=F32), 0.0)
        f_ref[...] += jnp.dot((hid * hid).astype(BF16), w2_ref[g], preferred_element_type=F32)
        return carry

    lax.fori_loop(0, groups, body, 0)
    o_ref[...] = o_ref[...] + mod_ref[0, 5:6, :] * f_ref[...]


def _ffn(x, attn, u, gates, mod, wo, cw, cb, lng, lnb, wpw, wout, g2, w1, w2):
    bsz, seq, d = x.shape
    t = FFN_TILE
    groups = t // CONV_GROUP
    tiles_per_seq = seq // t
    n = bsz * tiles_per_seq
    rows = bsz * seq
    flat = lambda a: a.reshape(rows, a.shape[-1])
    w1c = w1.reshape(d, groups, -1).transpose(1, 0, 2)
    w2c = w2.reshape(groups, -1, d)
    tok = lambda w: pl.BlockSpec((t, w), lambda i: (i, 0))
    next_tile = lambda i: jnp.minimum(i + 1, n - 1)
    kern = functools.partial(_ffn_kernel, tiles_per_seq=tiles_per_seq)
    out = pl.pallas_call(
        kern,
        out_shape=jax.ShapeDtypeStruct((rows, d), x.dtype),
        grid=(n,),
        in_specs=[
            tok(d), tok(attn.shape[-1]), tok(2 * d),
            pl.BlockSpec((1, ADA_CHUNKS, d), lambda i: (i // tiles_per_seq, 0, 0)),
            pl.BlockSpec((t, CONV_CH), lambda i: (0, 0), pipeline_mode=pl.Buffered(1)),
            pl.BlockSpec((t, CONV_CH), lambda i: (next_tile(i), 0)),
            pl.BlockSpec((HALO, CONV_CH), lambda i: (next_tile(i) * (t // HALO) - 1, 0)),
            _const_spec(wo.shape), _const_spec(cw.shape), _const_spec(cb.shape),
            _const_spec(lng.shape), _const_spec(lnb.shape), _const_spec(wpw.shape),
            _const_spec(wout.shape), _const_spec(g2.shape), _const_spec(w1c.shape),
            _const_spec(w2c.shape),
        ],
        out_specs=tok(d),
        scratch_shapes=[
            pltpu.VMEM((SUBLANES, CONV_GROUP + HALO, CONV_CH), F32),
            pltpu.VMEM((CONV_WIDTH, SUBLANES, CONV_CH), F32),
            pltpu.VMEM((2, t, CONV_CH), BF16),
            pltpu.VMEM((t, d), BF16),
            pltpu.VMEM((t, d), F32),
        ],
        compiler_params=pltpu.CompilerParams(
            dimension_semantics=("arbitrary",), vmem_limit_bytes=VMEM_LIMIT_BYTES),
        name="ffn",
    )(flat(x), flat(attn), flat(gates), mod, flat(u), flat(u), flat(u),
      wo, cw, cb, lng, lnb, wpw, wout, g2, w1c, w2c)
    return out.reshape(bsz, seq, d)


def _head_blocks(w, width):
    return w.reshape(w.shape[0], N_HEADS, width)


def _pad_heads(w3):
    k, h, w = w3.shape
    return jnp.pad(w3, ((0, 0), (0, 0), (0, HEAD_BLOCK - w))).reshape(k, h * HEAD_BLOCK)


def _swap_rope_halves(w3):
    nope = jnp.zeros_like(w3[..., :QK_NOPE_DIM])
    lo = w3[..., QK_NOPE_DIM:QK_NOPE_DIM + ROPE_HALF]
    hi = w3[..., QK_NOPE_DIM + ROPE_HALF:]
    return jnp.concatenate([nope, hi, lo], axis=-1)


def _rope_gain_tables(seq, gain, premul):
    inv_freq = ROPE_THETA ** (-jnp.arange(0, QK_ROPE_DIM, 2, dtype=F32) / QK_ROPE_DIM)
    ang = jnp.arange(seq, dtype=F32)[:, None] * inv_freq[None, :]
    cos, sin = jnp.cos(ang), jnp.sin(ang)
    pad = HEAD_BLOCK - QK_HEAD_DIM
    g_nope = gain[:QK_NOPE_DIM]
    g_lo = gain[QK_NOPE_DIM:QK_NOPE_DIM + ROPE_HALF]
    g_hi = gain[QK_NOPE_DIM + ROPE_HALF:]
    a = jnp.concatenate([jnp.broadcast_to(g_nope, (seq, QK_NOPE_DIM)), cos * g_lo, cos * g_hi,
                         jnp.zeros((seq, pad), F32)], axis=-1)
    b = jnp.concatenate([jnp.zeros((seq, QK_NOPE_DIM), F32), -sin * g_hi, sin * g_lo,
                         jnp.zeros((seq, pad), F32)], axis=-1)
    return a * premul, b * premul


def kernel(x, c, w_ada, b_ada, norm1_g, w_in, q_latent_g, w_uq, kv_latent_g, w_ukv, qk_norm_q_g,
           qk_norm_k_g, w_o_mla, conv_w, conv_b, conv_ln_g, conv_ln_b, w_pw_out, w_out, norm2_g,
           w_ff1, w_ff2):
    bsz, seq, d = x.shape
    depth = w_ada.shape[0]
    off_q = Q_LORA_RANK
    off_kv = off_q + KV_LORA_RANK
    off_kr = off_kv + QK_ROPE_DIM
    off_glu = off_kr + 2 * CONV_CH
    row = lambda v: v.reshape(1, -1)

    q_premul = (QK_HEAD_DIM ** -0.5) * math.log2(math.e)

    for l in range(depth):
        wi = w_in[l]
        kr_cols = wi[:, off_kv:off_kr]
        kr_swapped = jnp.concatenate([kr_cols[:, ROPE_HALF:], kr_cols[:, :ROPE_HALF]], axis=-1)
        wa = jnp.concatenate(
            [wi[:, :off_kr], kr_swapped,
             jnp.zeros((d, 4 * LANES - off_kr - QK_ROPE_DIM), wi.dtype)], axis=-1).astype(BF16)
        wglu = wi[:, off_kr:off_glu].astype(BF16)
        wgate = wi[:, off_glu:].astype(BF16)

        uq3 = _head_blocks(w_uq[l], QK_HEAD_DIM)
        wuq = _pad_heads(uq3).astype(BF16)
        wuqs = _pad_heads(_swap_rope_halves(uq3)).astype(BF16)

        ukv3 = _head_blocks(w_ukv[l], QK_NOPE_DIM + V_HEAD_DIM)
        wuk = _pad_heads(ukv3[..., :QK_NOPE_DIM]).astype(BF16)
        wuvt = jnp.pad(ukv3[..., QK_NOPE_DIM:], ((0, 0), (0, 0), (0, VT_ROWS - V_HEAD_DIM)))
        wuvt = wuvt.reshape(KV_LORA_RANK, N_HEADS * VT_ROWS).T.astype(BF16)

        aq, bq = _rope_gain_tables(seq, qk_norm_q_g[l], q_premul)
        ak, bk = _rope_gain_tables(seq, qk_norm_k_g[l], 1.0)

        mod = _adaln(c, w_ada[l], b_ada[l]).reshape(bsz, ADA_CHUNKS, d)
        q, k, vt, u, gates = _proj(
            x, mod, row(norm1_g[l]), wa, wglu, wgate, row(q_latent_g[l]), wuq, wuqs,
            row(kv_latent_g[l]), wuk, wuvt, aq, bq, ak, bk)
        attn = _attn(q, k, vt)
        x = _ffn(x, attn, u, gates, mod, w_o_mla[l].astype(BF16), conv_w[l], row(conv_b[l]),
                 row(conv_ln_g[l]), row(conv_ln_b[l]), w_pw_out[l].astype(BF16),
                 w_out[l].astype(BF16), row(norm2_g[l]), w_ff1[l].astype(BF16),
                 w_ff2[l].astype(BF16))
    return x
```

```python
import functools
import itertools
import math

import jax
import jax.numpy as jnp
from jax import lax
from jax.experimental import pallas as pl
from jax.experimental.pallas import tpu as pltpu

F32 = jnp.float32
BF16 = jnp.bfloat16

CHUNK = 64
N_HEADS = 8
QK_NOPE_DIM = 64
QK_ROPE_DIM = 32
QK_HEAD_DIM = QK_NOPE_DIM + QK_ROPE_DIM
V_HEAD_DIM = 64
Q_LORA_RANK = 256
KV_LORA_RANK = 128
CONV_CH = 512
CONV_WIDTH = 31
ADA_CHUNKS = 6
ROPE_THETA = 10000.0
EPS = 1e-6

LANES = 128
SUBLANES = 8
VMEM_LIMIT_BYTES = 56 * 1024 * 1024

HEAD_BLOCK = LANES
VT_ROWS = 80
HALO = 32
ROPE_HALF = QK_ROPE_DIM // 2

PROJ_TILE = 512
FFN_TILE = 512
ATTN_QBLOCK = 256
ATTN_KCHUNK = 512
CONV_ROWS = 32
CONV_ACC_BLOCKS = 2
CONV_WINDOWS = (3, 4, 4, 4, 1)
CONV_GROUP = 128


def _rms_scale(x, width):
    return lax.rsqrt(jnp.sum(x * x, axis=-1, keepdims=True) * (1.0 / width) + EPS)


def _adaln_kernel(c_ref, w_ref, b_ref, o_ref):
    c = c_ref[...]
    act = (c * jax.nn.sigmoid(c)).astype(BF16)
    o_ref[...] = jnp.dot(act, w_ref[...].astype(BF16), preferred_element_type=F32) + b_ref[...]


def _adaln(c, w, b):
    bsz, d = c.shape
    n = w.shape[1]
    tn = 1024
    return pl.pallas_call(
        _adaln_kernel,
        out_shape=jax.ShapeDtypeStruct((bsz, n), F32),
        grid=(n // tn,),
        in_specs=[
            pl.BlockSpec((bsz, d), lambda j: (0, 0)),
            pl.BlockSpec((d, tn), lambda j: (0, j)),
            pl.BlockSpec((1, tn), lambda j: (0, j)),
        ],
        out_specs=pl.BlockSpec((bsz, tn), lambda j: (0, j)),
        compiler_params=pltpu.CompilerParams(
            dimension_semantics=("arbitrary",), vmem_limit_bytes=VMEM_LIMIT_BYTES),
        name="adaln",
    )(c, w, b.reshape(1, n))


def _proj_kernel(x_ref, mod_ref, g1_ref, wa_ref, wglu_ref, wgate_ref, gq_ref, wuq_ref, wuqs_ref,
                 gkv_ref, wuk_ref, wuvt_ref, aq_ref, bq_ref, ak_ref, bk_ref,
                 q_ref, k_ref, vt_ref, u_ref, gate_ref):
    x = x_ref[0]
    scale_row = g1_ref[...] * (1.0 + mod_ref[0, 1:2, :])
    shift_row = mod_ref[0, 0:1, :]
    hb = (x * _rms_scale(x, x.shape[-1]) * scale_row + shift_row).astype(BF16)

    z_a = jnp.dot(hb, wa_ref[...], preferred_element_type=F32)

    zq = z_a[:, :Q_LORA_RANK]
    qn = (zq * _rms_scale(zq, Q_LORA_RANK) * gq_ref[...]).astype(BF16)
    q_pre = jnp.dot(qn, wuq_ref[...], preferred_element_type=F32)
    q_swp = jnp.dot(qn, wuqs_ref[...], preferred_element_type=F32)
    aq = aq_ref[...]
    bq = bq_ref[...]
    for h in range(N_HEADS):
        sl = slice(h * HEAD_BLOCK, (h + 1) * HEAD_BLOCK)
        blk = q_pre[:, sl]
        qh = _rms_scale(blk, QK_HEAD_DIM) * (blk * aq + q_swp[:, sl] * bq)
        q_ref[0, :, sl] = qh.astype(BF16)

    zkv = z_a[:, Q_LORA_RANK:Q_LORA_RANK + KV_LORA_RANK]
    kvn = (zkv * _rms_scale(zkv, KV_LORA_RANK) * gkv_ref[...]).astype(BF16)
    k_up = jnp.dot(kvn, wuk_ref[...], preferred_element_type=F32)

    kr_blk = z_a[:, Q_LORA_RANK + KV_LORA_RANK:]
    lane = lax.broadcasted_iota(jnp.int32, (1, LANES), 1)
    kr_only = jnp.where(lane < QK_ROPE_DIM, kr_blk, 0.0)
    ss_kr = jnp.sum(kr_only * kr_only, axis=-1, keepdims=True)
    ak = ak_ref[...]
    bk = bk_ref[...]
    kr_rot = (pltpu.roll(kr_blk, QK_NOPE_DIM, 1) * ak
              + pltpu.roll(kr_blk, QK_NOPE_DIM - QK_ROPE_DIM, 1) * bk)
    for h in range(N_HEADS):
        sl = slice(h * HEAD_BLOCK, (h + 1) * HEAD_BLOCK)
        blk = k_up[:, sl]
        ss = jnp.sum(blk * blk, axis=-1, keepdims=True) + ss_kr
        kh = lax.rsqrt(ss * (1.0 / QK_HEAD_DIM) + EPS) * (blk * ak + kr_rot)
        k_ref[0, :, sl] = kh.astype(BF16)

    vt = lax.dot_general(wuvt_ref[...], kvn, (((1,), (1,)), ((), ())), preferred_element_type=F32)
    row = lax.broadcasted_iota(jnp.int32, vt.shape, 0)
    vt_ref[0] = jnp.where(row % VT_ROWS == V_HEAD_DIM, 1.0, vt).astype(BF16)

    zg = jnp.dot(hb, wglu_ref[...], preferred_element_type=F32)
    u_ref[0] = zg[:, :CONV_CH] * jax.nn.sigmoid(zg[:, CONV_CH:])

    gate_ref[0] = jax.nn.sigmoid(
        jnp.dot(hb, wgate_ref[...], preferred_element_type=F32)).astype(BF16)


def _const_spec(shape):
    nd = len(shape)
    return pl.BlockSpec(shape, lambda *_: (0,) * nd, pipeline_mode=pl.Buffered(1))


def _proj(x, mod, g1, wa, wglu, wgate, gq, wuq, wuqs, gkv, wuk, wuvt, aq, bq, ak, bk):
    bsz, seq, d = x.shape
    t = PROJ_TILE
    d_qk = N_HEADS * HEAD_BLOCK
    vt_rows = N_HEADS * VT_ROWS
    tok = lambda w: pl.BlockSpec((1, t, w), lambda b, s: (b, s, 0))
    pos = lambda: pl.BlockSpec((t, LANES), lambda b, s: (s, 0))
    out_shapes = (
        jax.ShapeDtypeStruct((bsz, seq, d_qk), BF16),
        jax.ShapeDtypeStruct((bsz, seq, d_qk), BF16),
        jax.ShapeDtypeStruct((bsz, vt_rows, seq), BF16),
        jax.ShapeDtypeStruct((bsz, seq, CONV_CH), F32),
        jax.ShapeDtypeStruct((bsz, seq, 2 * d), BF16),
    )
    return pl.pallas_call(
        _proj_kernel,
        out_shape=out_shapes,
        grid=(bsz, seq // t),
        in_specs=[
            tok(d),
            pl.BlockSpec((1, ADA_CHUNKS, d), lambda b, s: (b, 0, 0)),
            _const_spec(g1.shape), _const_spec(wa.shape), _const_spec(wglu.shape),
            _const_spec(wgate.shape), _const_spec(gq.shape), _const_spec(wuq.shape),
            _const_spec(wuqs.shape), _const_spec(gkv.shape), _const_spec(wuk.shape),
            _const_spec(wuvt.shape), pos(), pos(), pos(), pos(),
        ],
        out_specs=(tok(d_qk), tok(d_qk),
                   pl.BlockSpec((1, vt_rows, t), lambda b, s: (b, 0, s)),
                   tok(CONV_CH), tok(2 * d)),
        compiler_params=pltpu.CompilerParams(
            dimension_semantics=("arbitrary", "arbitrary"), vmem_limit_bytes=VMEM_LIMIT_BYTES),
        name="proj",
    )(x, mod, g1, wa, wglu, wgate, gq, wuq, wuqs, gkv, wuk, wuvt, aq, bq, ak, bk)


def _attn_kernel(q_ref, k_ref, vt_ref, o_ref):
    seq = q_ref.shape[1]
    qb = ATTN_QBLOCK
    neg = jnp.finfo(F32).min
    key_chunk = lax.broadcasted_iota(jnp.int32, (qb, qb), 0) // CHUNK
    query_chunk = lax.broadcasted_iota(jnp.int32, (qb, qb), 1) // CHUNK
    diag_mask = key_chunk <= query_chunk
    contract_last = (((1,), (1,)), ((), ()))
    blocks = [(j, hh) for j in range(seq // qb) for hh in range(2)]

    def key_chunks(j):
        q0 = j * qb
        edges = list(range(0, q0, ATTN_KCHUNK)) + [q0]
        return [(a, b, False) for a, b in zip(edges[:-1], edges[1:])] + [(q0, q0 + qb, True)]

    def score_steps(j, hh, out):
        sl = slice(hh * HEAD_BLOCK, (hh + 1) * HEAD_BLOCK)
        qt = q_ref[0, j * qb:(j + 1) * qb, sl]
        m = None
        for a, b, diag in key_chunks(j):
            s = lax.dot_general(k_ref[0, a:b, sl], qt, contract_last, preferred_element_type=F32)
            if diag:
                s = jnp.where(diag_mask, s, neg)
            cm = jnp.max(s, axis=0, keepdims=True)
            m = cm if m is None else jnp.maximum(m, cm)
            out.append(s)
            yield
        out.append(m)

    def value_steps(j, hh, sc, out):
        rows = slice(hh * VT_ROWS, (hh + 1) * VT_ROWS)
        m = sc[-1]
        ot = None
        for (a, b, _), s in zip(key_chunks(j), sc[:-1]):
            part = jnp.dot(vt_ref[0, rows, a:b], jnp.exp2(s - m).astype(BF16),
                           preferred_element_type=F32)
            ot = part if ot is None else ot + part
            yield
        out.append(ot[:V_HEAD_DIM] / ot[V_HEAD_DIM:V_HEAD_DIM + 1])

    halves = []
    pending = []
    for _ in score_steps(*blocks[0], pending):
        pass
    for n, (j, hh) in enumerate(blocks):
        current, pending = pending, []
        nxt = score_steps(*blocks[n + 1], pending) if n + 1 < len(blocks) else iter(())
        cur = value_steps(j, hh, current, halves)
        live = [nxt, cur]
        while live:
            for it in list(live):
                if next(it, StopIteration) is StopIteration:
                    live.remove(it)
        if hh == 1:
            o_ref[0, j * qb:(j + 1) * qb, :] = jnp.concatenate(halves, axis=0).T.astype(BF16)
            halves = []


def _attn(q, k, vt):
    bsz, seq, d_qk = q.shape
    pair = 2 * HEAD_BLOCK
    spec = pl.BlockSpec((1, seq, pair), lambda b, p: (b, 0, p))
    return pl.pallas_call(
        _attn_kernel,
        out_shape=jax.ShapeDtypeStruct((bsz, seq, N_HEADS * V_HEAD_DIM), BF16),
        grid=(bsz, d_qk // pair),
        in_specs=[spec, spec, pl.BlockSpec((1, 2 * VT_ROWS, seq), lambda b, p: (b, p, 0))],
        out_specs=pl.BlockSpec((1, seq, 2 * V_HEAD_DIM), lambda b, p: (b, 0, p)),
        compiler_params=pltpu.CompilerParams(
            dimension_semantics=("arbitrary", "arbitrary"), vmem_limit_bytes=VMEM_LIMIT_BYTES),
        name="attn",
    )(q, k, vt)


def _conv_group(*args, **kwargs):
    for _ in _conv_group_steps(*args, **kwargs):
        pass


def _conv_group_steps(g, u_ref, halo, out_ref, ubuf, wbc, cb_ref, lng_ref, lnb_ref,
                      before_chunk=None, after_chunk=None):
    if isinstance(g, int):
        g0 = g * CONV_GROUP
        history = halo if g == 0 else u_ref[g0 - HALO:g0, :]
    else:
        g0 = pl.multiple_of(g * CONV_GROUP, CONV_GROUP)
        prev0 = pl.multiple_of(jnp.maximum(g0 - HALO, 0), HALO)
        history = jnp.where(g == 0, halo, u_ref[pl.ds(prev0, HALO), :])
    ext = CONV_GROUP + HALO
    ubuf[0, 0:HALO, :] = history
    ubuf[0, HALO:, :] = u_ref[pl.ds(g0, CONV_GROUP), :]
    for p in range(1, SUBLANES):
        ubuf[p, 0:ext - SUBLANES, :] = ubuf[0, p:ext - SUBLANES + p, :]

    tap0 = HALO - (CONV_WIDTH - 1)
    blocks = CONV_ROWS // SUBLANES
    for c in range(CONV_GROUP // CONV_ROWS):
        r0 = c * CONV_ROWS
        bias = jnp.broadcast_to(cb_ref[...], (SUBLANES, CONV_CH))
        token = None if before_chunk is None else before_chunk(c)
        if token is not None:
            bias = bias + jnp.concatenate([_unknown_zero(token)] * (CONV_CH // LANES), axis=1)
        acc = [bias] * blocks
        for rb0 in range(0, blocks, CONV_ACC_BLOCKS):
            for tap in range(CONV_WIDTH):
                blk, phase = divmod(tap0 + tap, SUBLANES)
                w = wbc[tap]
                for rb in range(rb0, rb0 + CONV_ACC_BLOCKS):
                    row = r0 + (blk + rb) * SUBLANES
                    acc[rb] = acc[rb] + ubuf[phase, row:row + SUBLANES, :] * w
        conv = jnp.concatenate(acc, axis=0)
        cen = conv - jnp.mean(conv, axis=-1, keepdims=True)
        var = jnp.mean(cen * cen, axis=-1, keepdims=True)
        y = cen * lax.rsqrt(var + EPS) * lng_ref[...] + lnb_ref[...]
        out_ref[pl.ds(g0 + r0, CONV_ROWS), :] = (y * jax.nn.sigmoid(y)).astype(BF16)
        if after_chunk is not None:
            after_chunk(c, y[0:SUBLANES, 0:LANES])
        yield


def _unknown_zero(tile):
    sixteen = jnp.uint32(16)
    bits = pltpu.bitcast(tile, jnp.uint32)
    zero_bits = lax.shift_right_logical(lax.shift_right_logical(bits, sixteen), sixteen)
    return pltpu.bitcast(zero_bits, F32)


def _order_after(ref, rows, tile):
    zero = _unknown_zero(tile)
    keep = ref[rows, 0:LANES].astype(F32) + jnp.concatenate([zero, zero], axis=0)
    ref[rows, 0:LANES] = keep.astype(BF16)


def _ffn_kernel(x_ref, attn_ref, gate_ref, mod_ref, ufirst_ref, unext_ref, halo_ref,
                wo_ref, cw_ref, cb_ref, lng_ref, lnb_ref, wpw_ref, wout_ref, g2_ref, w1_ref, w2_ref,
                o_ref, ubuf, wbc, cbuf, h2_ref, hid_ref, *, tiles_per_seq):
    t, d = x_ref.shape
    i = pl.program_id(0)
    slot = i % 2
    groups = t // CONV_GROUP
    conv_args = (ubuf, wbc, cb_ref, lng_ref, lnb_ref)

    @pl.when(i == 0)
    def _():
        for tap in range(CONV_WIDTH):
            wbc[tap] = jnp.broadcast_to(cw_ref[tap:tap + 1, :], (SUBLANES, CONV_CH))
        zero_halo = jnp.zeros((HALO, CONV_CH), F32)

        def first_tile(g, carry):
            _conv_group(g, ufirst_ref, zero_halo, cbuf.at[0], *conv_args)
            return carry

        lax.fori_loop(0, groups, first_tile, 0)

    half = t // 2
    row_halves = [slice(0, half), slice(half, t)]

    y_a = [jnp.dot(attn_ref[r, :], wo_ref[...], preferred_element_type=F32) for r in row_halves]
    y_b = [jnp.dot(cbuf[slot, r, :], wpw_ref[...], preferred_element_type=F32) for r in row_halves]
    mixed = []
    for r, ya, yb in zip(row_halves, y_a, y_b):
        mix = (gate_ref[r, :d].astype(F32) * ya + gate_ref[r, d:].astype(F32) * yb).astype(BF16)
        mixed.append(jnp.dot(mix, wout_ref[...], preferred_element_type=F32))

    next_starts_seq = (i + 1) % tiles_per_seq == 0
    halo_next = jnp.where(next_starts_seq, 0.0, halo_ref[...])

    pin_rows = [slice(r.start, r.start + 2 * SUBLANES) for r in row_halves]
    window_ends = list(itertools.accumulate(CONV_WINDOWS))
    window_pins = [(h2_ref, pin_rows[0]), (h2_ref, pin_rows[1]),
                   (hid_ref, pin_rows[0]), (hid_ref, pin_rows[1]), None]
    state = {"token": None, "done": 0}

    def before_chunk(c):
        return state["token"]

    def after_chunk(c, tile):
        state["done"] += 1
        if state["done"] in window_ends:
            pin = window_pins[window_ends.index(state["done"])]
            if pin is not None:
                _order_after(*pin, tile)

    def conv_steps():
        for g in range(groups):
            yield from _conv_group_steps(g, unext_ref, halo_next, cbuf.at[1 - slot], *conv_args,
                                         before_chunk=before_chunk, after_chunk=after_chunk)

    conv = conv_steps()

    def run_window(k, token):
        state["token"] = token
        while state["done"] < window_ends[k]:
            next(conv)

    ff_chunk = w1_ref.shape[2]

    def up_phase(r):
        for g in range(groups):
            up = jnp.maximum(jnp.dot(h2_ref[r, :], w1_ref[g], preferred_element_type=F32), 0.0)
            hid_ref[r, g * ff_chunk:(g + 1) * ff_chunk] = (up * up).astype(BF16)
        return up[0:SUBLANES, 0:LANES]

    scale_row = g2_ref[...] * (1.0 + mod_ref[0, 4:5, :])
    for r, mx in zip(row_halves, mixed):
        x1 = x_ref[r, :] + mod_ref[0, 2:3, :] * mx
        o_ref[r, :] = x1
        h2_ref[r, :] = (x1 * _rms_scale(x1, d) * scale_row + mod_ref[0, 3:4, :]).astype(BF16)

    run_window(0, None)
    run_window(1, mixed[1][0:SUBLANES, 0:LANES])
    token = up_phase(row_halves[0])
    run_window(2, token)
    token = up_phase(row_halves[1])
    run_window(3, token)
    for n, r in enumerate(row_halves):
        down = jnp.dot(hid_ref[r, :], w2_ref[...], preferred_element_type=F32)
        o_ref[r, :] = o_ref[r, :] + mod_ref[0, 5:6, :] * down
        if n == 0:
            run_window(4, down[0:SUBLANES, 0:LANES])


def _ffn(x, attn, u, gates, mod, wo, cw, cb, lng, lnb, wpw, wout, g2, w1, w2):
    bsz, seq, d = x.shape
    t = FFN_TILE
    groups = t // CONV_GROUP
    tiles_per_seq = seq // t
    n = bsz * tiles_per_seq
    rows = bsz * seq
    flat = lambda a: a.reshape(rows, a.shape[-1])
    w1c = w1.reshape(d, groups, -1).transpose(1, 0, 2)
    tok = lambda w: pl.BlockSpec((t, w), lambda i: (i, 0))
    next_tile = lambda i: jnp.minimum(i + 1, n - 1)
    kern = functools.partial(_ffn_kernel, tiles_per_seq=tiles_per_seq)
    out = pl.pallas_call(
        kern,
        out_shape=jax.ShapeDtypeStruct((rows, d), x.dtype),
        grid=(n,),
        in_specs=[
            tok(d), tok(attn.shape[-1]), tok(2 * d),
            pl.BlockSpec((1, ADA_CHUNKS, d), lambda i: (i // tiles_per_seq, 0, 0)),
            pl.BlockSpec((t, CONV_CH), lambda i: (0, 0), pipeline_mode=pl.Buffered(1)),
            pl.BlockSpec((t, CONV_CH), lambda i: (next_tile(i), 0)),
            pl.BlockSpec((HALO, CONV_CH), lambda i: (next_tile(i) * (t // HALO) - 1, 0)),
            _const_spec(wo.shape), _const_spec(cw.shape), _const_spec(cb.shape),
            _const_spec(lng.shape), _const_spec(lnb.shape), _const_spec(wpw.shape),
            _const_spec(wout.shape), _const_spec(g2.shape), _const_spec(w1c.shape),
            _const_spec(w2.shape),
        ],
        out_specs=tok(d),
        scratch_shapes=[
            pltpu.VMEM((SUBLANES, CONV_GROUP + HALO, CONV_CH), F32),
            pltpu.VMEM((CONV_WIDTH, SUBLANES, CONV_CH), F32),
            pltpu.VMEM((2, t, CONV_CH), BF16),
            pltpu.VMEM((t, d), BF16),
            pltpu.VMEM((t, w1.shape[1]), BF16),
        ],
        compiler_params=pltpu.CompilerParams(
            dimension_semantics=("arbitrary",), vmem_limit_bytes=VMEM_LIMIT_BYTES),
        name="ffn",
    )(flat(x), flat(attn), flat(gates), mod, flat(u), flat(u), flat(u),
      wo, cw, cb, lng, lnb, wpw, wout, g2, w1c, w2)
    return out.reshape(bsz, seq, d)


def _head_blocks(w, width):
    return w.reshape(w.shape[0], N_HEADS, width)


def _pad_heads(w3):
    k, h, w = w3.shape
    return jnp.pad(w3, ((0, 0), (0, 0), (0, HEAD_BLOCK - w))).reshape(k, h * HEAD_BLOCK)


def _swap_rope_halves(w3):
    nope = jnp.zeros_like(w3[..., :QK_NOPE_DIM])
    lo = w3[..., QK_NOPE_DIM:QK_NOPE_DIM + ROPE_HALF]
    hi = w3[..., QK_NOPE_DIM + ROPE_HALF:]
    return jnp.concatenate([nope, hi, lo], axis=-1)


def _rope_gain_tables(seq, gain, premul):
    inv_freq = ROPE_THETA ** (-jnp.arange(0, QK_ROPE_DIM, 2, dtype=F32) / QK_ROPE_DIM)
    ang = jnp.arange(seq, dtype=F32)[:, None] * inv_freq[None, :]
    cos, sin = jnp.cos(ang), jnp.sin(ang)
    pad = HEAD_BLOCK - QK_HEAD_DIM
    g_nope = gain[:QK_NOPE_DIM]
    g_lo = gain[QK_NOPE_DIM:QK_NOPE_DIM + ROPE_HALF]
    g_hi = gain[QK_NOPE_DIM + ROPE_HALF:]
    a = jnp.concatenate([jnp.broadcast_to(g_nope, (seq, QK_NOPE_DIM)), cos * g_lo, cos * g_hi,
                         jnp.zeros((seq, pad), F32)], axis=-1)
    b = jnp.concatenate([jnp.zeros((seq, QK_NOPE_DIM), F32), -sin * g_hi, sin * g_lo,
                         jnp.zeros((seq, pad), F32)], axis=-1)
    return a * premul, b * premul


def kernel(x, c, w_ada, b_ada, norm1_g, w_in, q_latent_g, w_uq, kv_latent_g, w_ukv, qk_norm_q_g,
           qk_norm_k_g, w_o_mla, conv_w, conv_b, conv_ln_g, conv_ln_b, w_pw_out, w_out, norm2_g,
           w_ff1, w_ff2):
    bsz, seq, d = x.shape
    depth = w_ada.shape[0]
    off_q = Q_LORA_RANK
    off_kv = off_q + KV_LORA_RANK
    off_kr = off_kv + QK_ROPE_DIM
    off_glu = off_kr + 2 * CONV_CH
    row = lambda v: v.reshape(1, -1)

    q_premul = (QK_HEAD_DIM ** -0.5) * math.log2(math.e)

    for l in range(depth):
        wi = w_in[l]
        kr_cols = wi[:, off_kv:off_kr]
        kr_swapped = jnp.concatenate([kr_cols[:, ROPE_HALF:], kr_cols[:, :ROPE_HALF]], axis=-1)
        wa = jnp.concatenate(
            [wi[:, :off_kr], kr_swapped,
             jnp.zeros((d, 4 * LANES - off_kr - QK_ROPE_DIM), wi.dtype)], axis=-1).astype(BF16)
        wglu = wi[:, off_kr:off_glu].astype(BF16)
        wgate = wi[:, off_glu:].astype(BF16)

        uq3 = _head_blocks(w_uq[l], QK_HEAD_DIM)
        wuq = _pad_heads(uq3).astype(BF16)
        wuqs = _pad_heads(_swap_rope_halves(uq3)).astype(BF16)

        ukv3 = _head_blocks(w_ukv[l], QK_NOPE_DIM + V_HEAD_DIM)
        wuk = _pad_heads(ukv3[..., :QK_NOPE_DIM]).astype(BF16)
        wuvt = jnp.pad(ukv3[..., QK_NOPE_DIM:], ((0, 0), (0, 0), (0, VT_ROWS - V_HEAD_DIM)))
        wuvt = wuvt.reshape(KV_LORA_RANK, N_HEADS * VT_ROWS).T.astype(BF16)

        aq, bq = _rope_gain_tables(seq, qk_norm_q_g[l], q_premul)
        ak, bk = _rope_gain_tables(seq, qk_norm_k_g[l], 1.0)

        mod = _adaln(c, w_ada[l], b_ada[l]).reshape(bsz, ADA_CHUNKS, d)
        q, k, vt, u, gates = _proj(
            x, mod, row(norm1_g[l]), wa, wglu, wgate, row(q_latent_g[l]), wuq, wuqs,
            row(kv_latent_g[l]), wuk, wuvt, aq, bq, ak, bk)
        attn = _attn(q, k, vt)
        x = _ffn(x, attn, u, gates, mod, w_o_mla[l].astype(BF16), conv_w[l], row(conv_b[l]),
                 row(conv_ln_g[l]), row(conv_ln_b[l]), w_pw_out[l].astype(BF16),
                 w_out[l].astype(BF16), row(norm2_g[l]), w_ff1[l].astype(BF16),
                 w_ff2[l].astype(BF16))
    return x
```

```python
import functools
import itertools
import math

import jax
import jax.numpy as jnp
from jax import lax
from jax.experimental import pallas as pl
from jax.experimental.pallas import tpu as pltpu

F32 = jnp.float32
BF16 = jnp.bfloat16

CHUNK = 64
N_HEADS = 8
QK_NOPE_DIM = 64
QK_ROPE_DIM = 32
QK_HEAD_DIM = QK_NOPE_DIM + QK_ROPE_DIM
V_HEAD_DIM = 64
Q_LORA_RANK = 256
KV_LORA_RANK = 128
CONV_CH = 512
CONV_WIDTH = 31
ADA_CHUNKS = 6
ROPE_THETA = 10000.0
EPS = 1e-6

LANES = 128
SUBLANES = 8
VMEM_LIMIT_BYTES = 56 * 1024 * 1024

HEAD_BLOCK = LANES
VT_ROWS = 80
HALO = 32
ROPE_HALF = QK_ROPE_DIM // 2

PROJ_TILE = 512
FFN_TILE = 512
ATTN_QBLOCK = 256
ATTN_KCHUNK = 1024
CONV_ROWS = 32
CONV_ACC_BLOCKS = 2
CONV_WINDOWS = (3, 4, 4, 4, 1)
CONV_GROUP = 128
FF_CHUNK = 1024


def _sigmoid(x):
    return 0.5 * jnp.tanh(0.5 * x) + 0.5


def _rms_scale(x, width):
    return lax.rsqrt(jnp.sum(x * x, axis=-1, keepdims=True) * (1.0 / width) + EPS)


def _adaln_kernel(c_ref, w_ref, b_ref, o_ref):
    c = c_ref[...]
    act = (c * _sigmoid(c)).astype(BF16)
    o_ref[...] = jnp.dot(act, w_ref[...].astype(BF16), preferred_element_type=F32) + b_ref[...]


def _adaln(c, w, b):
    bsz, d = c.shape
    n = w.shape[1]
    tn = 1024
    return pl.pallas_call(
        _adaln_kernel,
        out_shape=jax.ShapeDtypeStruct((bsz, n), F32),
        grid=(n // tn,),
        in_specs=[
            pl.BlockSpec((bsz, d), lambda j: (0, 0)),
            pl.BlockSpec((d, tn), lambda j: (0, j)),
            pl.BlockSpec((1, tn), lambda j: (0, j)),
        ],
        out_specs=pl.BlockSpec((bsz, tn), lambda j: (0, j)),
        compiler_params=pltpu.CompilerParams(
            dimension_semantics=("arbitrary",), vmem_limit_bytes=VMEM_LIMIT_BYTES),
        name="adaln",
    )(c, w, b.reshape(1, n))


def _proj_kernel(x_ref, mod_ref, g1_ref, wa_ref, wglu_ref, wgate_ref, gq_ref, wuq_ref, wuqs_ref,
                 gkv_ref, wuk_ref, wuvt_ref, aq_ref, bq_ref, ak_ref, bk_ref,
                 q_ref, k_ref, vt_ref, u_ref, gate_ref):
    def half_tile(r):
        x = x_ref[0, r, :]
        scale_row = g1_ref[...] * (1.0 + mod_ref[0, 1:2, :])
        shift_row = mod_ref[0, 0:1, :]
        hb = (x * _rms_scale(x, x.shape[-1]) * scale_row + shift_row).astype(BF16)

        z_a = jnp.dot(hb, wa_ref[...], preferred_element_type=F32)

        zq = z_a[:, :Q_LORA_RANK]
        qn = (zq * _rms_scale(zq, Q_LORA_RANK) * gq_ref[...]).astype(BF16)
        q_pre = jnp.dot(qn, wuq_ref[...], preferred_element_type=F32)
        q_swp = jnp.dot(qn, wuqs_ref[...], preferred_element_type=F32)
        aq = aq_ref[r, :]
        bq = bq_ref[r, :]
        for h in range(N_HEADS):
            sl = slice(h * HEAD_BLOCK, (h + 1) * HEAD_BLOCK)
            blk = q_pre[:, sl]
            qh = _rms_scale(blk, QK_HEAD_DIM) * (blk * aq + q_swp[:, sl] * bq)
            q_ref[0, r, sl] = qh.astype(BF16)

        zkv = z_a[:, Q_LORA_RANK:Q_LORA_RANK + KV_LORA_RANK]
        kvn = (zkv * _rms_scale(zkv, KV_LORA_RANK) * gkv_ref[...]).astype(BF16)
        k_up = jnp.dot(kvn, wuk_ref[...], preferred_element_type=F32)

        kr_blk = z_a[:, Q_LORA_RANK + KV_LORA_RANK:]
        lane = lax.broadcasted_iota(jnp.int32, (1, LANES), 1)
        kr_only = jnp.where(lane < QK_ROPE_DIM, kr_blk, 0.0)
        ss_kr = jnp.sum(kr_only * kr_only, axis=-1, keepdims=True)
        ak = ak_ref[r, :]
        bk = bk_ref[r, :]
        kr_rot = (pltpu.roll(kr_blk, QK_NOPE_DIM, 1) * ak
                  + pltpu.roll(kr_blk, QK_NOPE_DIM - QK_ROPE_DIM, 1) * bk)
        for h in range(N_HEADS):
            sl = slice(h * HEAD_BLOCK, (h + 1) * HEAD_BLOCK)
            blk = k_up[:, sl]
            ss = jnp.sum(blk * blk, axis=-1, keepdims=True) + ss_kr
            kh = lax.rsqrt(ss * (1.0 / QK_HEAD_DIM) + EPS) * (blk * ak + kr_rot)
            k_ref[0, r, sl] = kh.astype(BF16)

        vt = lax.dot_general(wuvt_ref[...], kvn, (((1,), (1,)), ((), ())), preferred_element_type=F32)
        row = lax.broadcasted_iota(jnp.int32, vt.shape, 0)
        vt_ref[0, :, r] = jnp.where(row % VT_ROWS == V_HEAD_DIM, 1.0, vt).astype(BF16)

        zg = jnp.dot(hb, wglu_ref[...], preferred_element_type=F32)
        u_ref[0, r, :] = zg[:, :CONV_CH] * _sigmoid(zg[:, CONV_CH:])

        gate_ref[0, r, :] = _sigmoid(
            jnp.dot(hb, wgate_ref[...], preferred_element_type=F32)).astype(BF16)

    t = x_ref.shape[1]
    for r in (slice(0, t // 2), slice(t // 2, t)):
        half_tile(r)


def _const_spec(shape):
    nd = len(shape)
    return pl.BlockSpec(shape, lambda *_: (0,) * nd, pipeline_mode=pl.Buffered(1))


def _proj(x, mod, g1, wa, wglu, wgate, gq, wuq, wuqs, gkv, wuk, wuvt, aq, bq, ak, bk):
    bsz, seq, d = x.shape
    t = PROJ_TILE
    d_qk = N_HEADS * HEAD_BLOCK
    vt_rows = N_HEADS * VT_ROWS
    tok = lambda w: pl.BlockSpec((1, t, w), lambda b, s: (b, s, 0))
    pos = lambda: pl.BlockSpec((t, LANES), lambda b, s: (s, 0))
    out_shapes = (
        jax.ShapeDtypeStruct((bsz, seq, d_qk), BF16),
        jax.ShapeDtypeStruct((bsz, seq, d_qk), BF16),
        jax.ShapeDtypeStruct((bsz, vt_rows, seq), BF16),
        jax.ShapeDtypeStruct((bsz, seq, CONV_CH), F32),
        jax.ShapeDtypeStruct((bsz, seq, 2 * d), BF16),
    )
    return pl.pallas_call(
        _proj_kernel,
        out_shape=out_shapes,
        grid=(bsz, seq // t),
        in_specs=[
            tok(d),
            pl.BlockSpec((1, ADA_CHUNKS, d), lambda b, s: (b, 0, 0)),
            _const_spec(g1.shape), _const_spec(wa.shape), _const_spec(wglu.shape),
            _const_spec(wgate.shape), _const_spec(gq.shape), _const_spec(wuq.shape),
            _const_spec(wuqs.shape), _const_spec(gkv.shape), _const_spec(wuk.shape),
            _const_spec(wuvt.shape), pos(), pos(), pos(), pos(),
        ],
        out_specs=(tok(d_qk), tok(d_qk),
                   pl.BlockSpec((1, vt_rows, t), lambda b, s: (b, 0, s)),
                   tok(CONV_CH), tok(2 * d)),
        compiler_params=pltpu.CompilerParams(
            dimension_semantics=("arbitrary", "arbitrary"), vmem_limit_bytes=VMEM_LIMIT_BYTES),
        name="proj",
    )(x, mod, g1, wa, wglu, wgate, gq, wuq, wuqs, gkv, wuk, wuvt, aq, bq, ak, bk)


def _attn_kernel(q_ref, k_ref, vt_ref, o_ref):
    seq = q_ref.shape[1]
    qb = ATTN_QBLOCK
    neg = jnp.finfo(F32).min
    key_chunk = lax.broadcasted_iota(jnp.int32, (qb, qb), 0) // CHUNK
    query_chunk = lax.broadcasted_iota(jnp.int32, (qb, qb), 1) // CHUNK
    diag_mask = key_chunk <= query_chunk
    contract_last = (((1,), (1,)), ((), ()))
    blocks = [(j, hh) for j in range(seq // qb) for hh in range(2)]

    def key_chunks(j):
        q0 = j * qb
        edges = list(range(0, q0, ATTN_KCHUNK)) + [q0]
        return [(a, b, False) for a, b in zip(edges[:-1], edges[1:])] + [(q0, q0 + qb, True)]

    def score_steps(j, hh, out):
        sl = slice(hh * HEAD_BLOCK, (hh + 1) * HEAD_BLOCK)
        qt = q_ref[0, j * qb:(j + 1) * qb, sl]
        m = None
        for a, b, diag in key_chunks(j):
            s = lax.dot_general(k_ref[0, a:b, sl], qt, contract_last, preferred_element_type=F32)
            if diag:
                s = jnp.where(diag_mask, s, neg)
            cm = jnp.max(s, axis=0, keepdims=True)
            m = cm if m is None else jnp.maximum(m, cm)
            out.append(s)
            yield
        out.append(m)

    def value_steps(j, hh, sc, out):
        rows = slice(hh * VT_ROWS, (hh + 1) * VT_ROWS)
        m = sc[-1]
        ot = None
        for (a, b, _), s in zip(key_chunks(j), sc[:-1]):
            part = jnp.dot(vt_ref[0, rows, a:b], jnp.exp2(s - m).astype(BF16),
                           preferred_element_type=F32)
            ot = part if ot is None else ot + part
            yield
        out.append(ot[:V_HEAD_DIM] / ot[V_HEAD_DIM:V_HEAD_DIM + 1])

    halves = []
    pending = []
    for _ in score_steps(*blocks[0], pending):
        pass
    for n, (j, hh) in enumerate(blocks):
        current, pending = pending, []
        nxt = score_steps(*blocks[n + 1], pending) if n + 1 < len(blocks) else iter(())
        cur = value_steps(j, hh, current, halves)
        live = [nxt, cur]
        while live:
            for it in list(live):
                if next(it, StopIteration) is StopIteration:
                    live.remove(it)
        if hh == 1:
            o_ref[0, j * qb:(j + 1) * qb, :] = jnp.concatenate(halves, axis=0).T.astype(BF16)
            halves = []


def _attn(q, k, vt):
    bsz, seq, d_qk = q.shape
    pair = 2 * HEAD_BLOCK
    spec = pl.BlockSpec((1, seq, pair), lambda b, p: (b, 0, p))
    return pl.pallas_call(
        _attn_kernel,
        out_shape=jax.ShapeDtypeStruct((bsz, seq, N_HEADS * V_HEAD_DIM), BF16),
        grid=(bsz, d_qk // pair),
        in_specs=[spec, spec, pl.BlockSpec((1, 2 * VT_ROWS, seq), lambda b, p: (b, p, 0))],
        out_specs=pl.BlockSpec((1, seq, 2 * V_HEAD_DIM), lambda b, p: (b, 0, p)),
        compiler_params=pltpu.CompilerParams(
            dimension_semantics=("arbitrary", "arbitrary"), vmem_limit_bytes=VMEM_LIMIT_BYTES),
        name="attn",
    )(q, k, vt)


def _conv_group(*args, **kwargs):
    for _ in _conv_group_steps(*args, **kwargs):
        pass


def _conv_group_steps(g, u_ref, halo, out_ref, ubuf, wbc, cb_ref, lng_ref, lnb_ref,
                      before_chunk=None, after_chunk=None):
    if isinstance(g, int):
        g0 = g * CONV_GROUP
        history = halo if g == 0 else u_ref[g0 - HALO:g0, :]
    else:
        g0 = pl.multiple_of(g * CONV_GROUP, CONV_GROUP)
        prev0 = pl.multiple_of(jnp.maximum(g0 - HALO, 0), HALO)
        history = jnp.where(g == 0, halo, u_ref[pl.ds(prev0, HALO), :])
    ext = CONV_GROUP + HALO
    ubuf[0, 0:HALO, :] = history
    ubuf[0, HALO:, :] = u_ref[pl.ds(g0, CONV_GROUP), :]
    for p in range(1, SUBLANES):
        ubuf[p, 0:ext - SUBLANES, :] = ubuf[0, p:ext - SUBLANES + p, :]

    tap0 = HALO - (CONV_WIDTH - 1)
    blocks = CONV_ROWS // SUBLANES
    for c in range(CONV_GROUP // CONV_ROWS):
        r0 = c * CONV_ROWS
        bias = jnp.broadcast_to(cb_ref[...], (SUBLANES, CONV_CH))
        token = None if before_chunk is None else before_chunk(c)
        if token is not None:
            bias = bias + jnp.concatenate([_unknown_zero(token)] * (CONV_CH // LANES), axis=1)
        acc = [bias] * blocks
        for rb0 in range(0, blocks, CONV_ACC_BLOCKS):
            for tap in range(CONV_WIDTH):
                blk, phase = divmod(tap0 + tap, SUBLANES)
                w = wbc[tap]
                for rb in range(rb0, rb0 + CONV_ACC_BLOCKS):
                    row = r0 + (blk + rb) * SUBLANES
                    acc[rb] = acc[rb] + ubuf[phase, row:row + SUBLANES, :] * w
        conv = jnp.concatenate(acc, axis=0)
        cen = conv - jnp.mean(conv, axis=-1, keepdims=True)
        var = jnp.mean(cen * cen, axis=-1, keepdims=True)
        y = cen * lax.rsqrt(var + EPS) * lng_ref[...] + lnb_ref[...]
        out_ref[pl.ds(g0 + r0, CONV_ROWS), :] = (y * _sigmoid(y)).astype(BF16)
        if after_chunk is not None:
            after_chunk(c, y[0:SUBLANES, 0:LANES])
        yield


def _unknown_zero(tile):
    sixteen = jnp.uint32(16)
    bits = pltpu.bitcast(tile, jnp.uint32)
    zero_bits = lax.shift_right_logical(lax.shift_right_logical(bits, sixteen), sixteen)
    return pltpu.bitcast(zero_bits, F32)


def _order_after(ref, rows, tile):
    zero = _unknown_zero(tile)
    keep = ref[rows, 0:LANES].astype(F32) + jnp.concatenate([zero, zero], axis=0)
    ref[rows, 0:LANES] = keep.astype(BF16)


def _ffn_kernel(x_ref, attn_ref, gate_ref, mod_ref, ufirst_ref, unext_ref, halo_ref,
                wo_ref, cw_ref, cb_ref, lng_ref, lnb_ref, wpw_ref, wout_ref, g2_ref, w1_ref, w2_ref,
                o_ref, ubuf, wbc, cbuf, h2_ref, hid_ref, *, tiles_per_seq):
    t, d = x_ref.shape
    i = pl.program_id(0)
    slot = i % 2
    groups = t // CONV_GROUP
    conv_args = (ubuf, wbc, cb_ref, lng_ref, lnb_ref)

    @pl.when(i == 0)
    def _():
        for tap in range(CONV_WIDTH):
            wbc[tap] = jnp.broadcast_to(cw_ref[tap:tap + 1, :], (SUBLANES, CONV_CH))
        zero_halo = jnp.zeros((HALO, CONV_CH), F32)

        def first_tile(g, carry):
            _conv_group(g, ufirst_ref, zero_halo, cbuf.at[0], *conv_args)
            return carry

        lax.fori_loop(0, groups, first_tile, 0)

    half = t // 2
    row_halves = [slice(0, half), slice(half, t)]

    y_a = [jnp.dot(attn_ref[r, :], wo_ref[...], preferred_element_type=F32) for r in row_halves]
    y_b = [jnp.dot(cbuf[slot, r, :], wpw_ref[...], preferred_element_type=F32) for r in row_halves]
    mixed = []
    for r, ya, yb in zip(row_halves, y_a, y_b):
        mix = (gate_ref[r, :d].astype(F32) * ya + gate_ref[r, d:].astype(F32) * yb).astype(BF16)
        mixed.append(jnp.dot(mix, wout_ref[...], preferred_element_type=F32))

    next_starts_seq = (i + 1) % tiles_per_seq == 0
    halo_next = jnp.where(next_starts_seq, 0.0, halo_ref[...])

    pin_rows = [slice(r.start, r.start + 2 * SUBLANES) for r in row_halves]
    window_ends = list(itertools.accumulate(CONV_WINDOWS))
    window_pins = [(h2_ref, pin_rows[0]), (h2_ref, pin_rows[1]),
                   (hid_ref, pin_rows[0]), (hid_ref, pin_rows[1]), None]
    state = {"token": None, "done": 0}

    def before_chunk(c):
        return state["token"]

    def after_chunk(c, tile):
        state["done"] += 1
        if state["done"] in window_ends:
            pin = window_pins[window_ends.index(state["done"])]
            if pin is not None:
                _order_after(*pin, tile)

    def conv_steps():
        for g in range(groups):
            yield from _conv_group_steps(g, unext_ref, halo_next, cbuf.at[1 - slot], *conv_args,
                                         before_chunk=before_chunk, after_chunk=after_chunk)

    conv = conv_steps()

    def run_window(k, token):
        state["token"] = token
        while state["done"] < window_ends[k]:
            next(conv)

    def up_phase(r):
        for c in range(0, w1_ref.shape[1], FF_CHUNK):
            cols = slice(c, c + FF_CHUNK)
            up = jnp.maximum(jnp.dot(h2_ref[r, :], w1_ref[:, cols], preferred_element_type=F32), 0.0)
            hid_ref[r, cols] = (up * up).astype(BF16)
        return up[0:SUBLANES, 0:LANES]

    scale_row = g2_ref[...] * (1.0 + mod_ref[0, 4:5, :])
    for r, mx in zip(row_halves, mixed):
        x1 = x_ref[r, :] + mod_ref[0, 2:3, :] * mx
        o_ref[r, :] = x1
        h2_ref[r, :] = (x1 * _rms_scale(x1, d) * scale_row + mod_ref[0, 3:4, :]).astype(BF16)

    run_window(0, None)
    run_window(1, mixed[1][0:SUBLANES, 0:LANES])
    token = up_phase(row_halves[0])
    run_window(2, token)
    token = up_phase(row_halves[1])
    run_window(3, token)
    for n, r in enumerate(row_halves):
        down = jnp.dot(hid_ref[r, :], w2_ref[...], preferred_element_type=F32)
        o_ref[r, :] = o_ref[r, :] + mod_ref[0, 5:6, :] * down
        if n == 0:
            run_window(4, down[0:SUBLANES, 0:LANES])


def _ffn(x, attn, u, gates, mod, wo, cw, cb, lng, lnb, wpw, wout, g2, w1, w2):
    bsz, seq, d = x.shape
    t = FFN_TILE
    tiles_per_seq = seq // t
    n = bsz * tiles_per_seq
    rows = bsz * seq
    flat = lambda a: a.reshape(rows, a.shape[-1])
    tok = lambda w: pl.BlockSpec((t, w), lambda i: (i, 0))
    next_tile = lambda i: jnp.minimum(i + 1, n - 1)
    kern = functools.partial(_ffn_kernel, tiles_per_seq=tiles_per_seq)
    out = pl.pallas_call(
        kern,
        out_shape=jax.ShapeDtypeStruct((rows, d), x.dtype),
        grid=(n,),
        in_specs=[
            tok(d), tok(attn.shape[-1]), tok(2 * d),
            pl.BlockSpec((1, ADA_CHUNKS, d), lambda i: (i // tiles_per_seq, 0, 0)),
            pl.BlockSpec((t, CONV_CH), lambda i: (0, 0), pipeline_mode=pl.Buffered(1)),
            pl.BlockSpec((t, CONV_CH), lambda i: (next_tile(i), 0)),
            pl.BlockSpec((HALO, CONV_CH), lambda i: (next_tile(i) * (t // HALO) - 1, 0)),
            _const_spec(wo.shape), _const_spec(cw.shape), _const_spec(cb.shape),
            _const_spec(lng.shape), _const_spec(lnb.shape), _const_spec(wpw.shape),
            _const_spec(wout.shape), _const_spec(g2.shape), _const_spec(w1.shape),
            _const_spec(w2.shape),
        ],
        out_specs=tok(d),
        scratch_shapes=[
            pltpu.VMEM((SUBLANES, CONV_GROUP + HALO, CONV_CH), F32),
            pltpu.VMEM((CONV_WIDTH, SUBLANES, CONV_CH), F32),
            pltpu.VMEM((2, t, CONV_CH), BF16),
            pltpu.VMEM((t, d), BF16),
            pltpu.VMEM((t, w1.shape[1]), BF16),
        ],
        compiler_params=pltpu.CompilerParams(
            dimension_semantics=("arbitrary",), vmem_limit_bytes=VMEM_LIMIT_BYTES),
        name="ffn",
    )(flat(x), flat(attn), flat(gates), mod, flat(u), flat(u), flat(u),
      wo, cw, cb, lng, lnb, wpw, wout, g2, w1, w2)
    return out.reshape(bsz, seq, d)


def _head_blocks(w, width):
    return w.reshape(w.shape[0], N_HEADS, width)


def _pad_heads(w3):
    k, h, w = w3.shape
    return jnp.pad(w3, ((0, 0), (0, 0), (0, HEAD_BLOCK - w))).reshape(k, h * HEAD_BLOCK)


def _swap_rope_halves(w3):
    nope = jnp.zeros_like(w3[..., :QK_NOPE_DIM])
    lo = w3[..., QK_NOPE_DIM:QK_NOPE_DIM + ROPE_HALF]
    hi = w3[..., QK_NOPE_DIM + ROPE_HALF:]
    return jnp.concatenate([nope, hi, lo], axis=-1)


def _rope_gain_tables(seq, gain, premul):
    inv_freq = ROPE_THETA ** (-jnp.arange(0, QK_ROPE_DIM, 2, dtype=F32) / QK_ROPE_DIM)
    ang = jnp.arange(seq, dtype=F32)[:, None] * inv_freq[None, :]
    cos, sin = jnp.cos(ang), jnp.sin(ang)
    pad = HEAD_BLOCK - QK_HEAD_DIM
    g_nope = gain[:QK_NOPE_DIM]
    g_lo = gain[QK_NOPE_DIM:QK_NOPE_DIM + ROPE_HALF]
    g_hi = gain[QK_NOPE_DIM + ROPE_HALF:]
    a = jnp.concatenate([jnp.broadcast_to(g_nope, (seq, QK_NOPE_DIM)), cos * g_lo, cos * g_hi,
                         jnp.zeros((seq, pad), F32)], axis=-1)
    b = jnp.concatenate([jnp.zeros((seq, QK_NOPE_DIM), F32), -sin * g_hi, sin * g_lo,
                         jnp.zeros((seq, pad), F32)], axis=-1)
    return a * premul, b * premul


def kernel(x, c, w_ada, b_ada, norm1_g, w_in, q_latent_g, w_uq, kv_latent_g, w_ukv, qk_norm_q_g,
           qk_norm_k_g, w_o_mla, conv_w, conv_b, conv_ln_g, conv_ln_b, w_pw_out, w_out, norm2_g,
           w_ff1, w_ff2):
    bsz, seq, d = x.shape
    depth = w_ada.shape[0]
    off_q = Q_LORA_RANK
    off_kv = off_q + KV_LORA_RANK
    off_kr = off_kv + QK_ROPE_DIM
    off_glu = off_kr + 2 * CONV_CH
    row = lambda v: v.reshape(1, -1)

    q_premul = (QK_HEAD_DIM ** -0.5) * math.log2(math.e)

    for l in range(depth):
        wi = w_in[l]
        kr_cols = wi[:, off_kv:off_kr]
        kr_swapped = jnp.concatenate([kr_cols[:, ROPE_HALF:], kr_cols[:, :ROPE_HALF]], axis=-1)
        wa = jnp.concatenate(
            [wi[:, :off_kr], kr_swapped,
             jnp.zeros((d, 4 * LANES - off_kr - QK_ROPE_DIM), wi.dtype)], axis=-1).astype(BF16)
        wglu = wi[:, off_kr:off_glu].astype(BF16)
        wgate = wi[:, off_glu:].astype(BF16)

        uq3 = _head_blocks(w_uq[l], QK_HEAD_DIM)
        wuq = _pad_heads(uq3).astype(BF16)
        wuqs = _pad_heads(_swap_rope_halves(uq3)).astype(BF16)

        ukv3 = _head_blocks(w_ukv[l], QK_NOPE_DIM + V_HEAD_DIM)
        wuk = _pad_heads(ukv3[..., :QK_NOPE_DIM]).astype(BF16)
        wuvt = jnp.pad(ukv3[..., QK_NOPE_DIM:], ((0, 0), (0, 0), (0, VT_ROWS - V_HEAD_DIM)))
        wuvt = wuvt.reshape(KV_LORA_RANK, N_HEADS * VT_ROWS).T.astype(BF16)

        aq, bq = _rope_gain_tables(seq, qk_norm_q_g[l], q_premul)
        ak, bk = _rope_gain_tables(seq, qk_norm_k_g[l], 1.0)

        mod = _adaln(c, w_ada[l], b_ada[l]).reshape(bsz, ADA_CHUNKS, d)
        q, k, vt, u, gates = _proj(
            x, mod, row(norm1_g[l]), wa, wglu, wgate, row(q_latent_g[l]), wuq, wuqs,
            row(kv_latent_g[l]), wuk, wuvt, aq, bq, ak, bk)
        attn = _attn(q, k, vt)
        x = _ffn(x, attn, u, gates, mod, w_o_mla[l].astype(BF16), conv_w[l], row(conv_b[l]),
                 row(conv_ln_g[l]), row(conv_ln_b[l]), w_pw_out[l].astype(BF16),
                 w_out[l].astype(BF16), row(norm2_g[l]), w_ff1[l].astype(BF16),
                 w_ff2[l].astype(BF16))
    return x
```

```python
import functools
import itertools
import math

import jax
import jax.numpy as jnp
from jax import lax
from jax.experimental import pallas as pl
from jax.experimental.pallas import tpu as pltpu

F32 = jnp.float32
BF16 = jnp.bfloat16

CHUNK = 64
N_HEADS = 8
QK_NOPE_DIM = 64
QK_ROPE_DIM = 32
QK_HEAD_DIM = QK_NOPE_DIM + QK_ROPE_DIM
V_HEAD_DIM = 64
Q_LORA_RANK = 256
KV_LORA_RANK = 128
CONV_CH = 512
CONV_WIDTH = 31
ADA_CHUNKS = 6
ROPE_THETA = 10000.0
EPS = 1e-6

LANES = 128
SUBLANES = 8
VMEM_LIMIT_BYTES = 56 * 1024 * 1024

HEAD_BLOCK = LANES
VT_ROWS = 80
HALO = 32
ROPE_HALF = QK_ROPE_DIM // 2

PROJ_TILE = 512
FFN_TILE = 512
ATTN_QBLOCK = 256
ATTN_KCHUNK = 1024
CONV_ROWS = 32
CONV_ACC_BLOCKS = 2
CONV_WINDOWS = (3, 4, 4, 4, 1)
CONV_GROUP = 128
FF_CHUNK = 1024


def _sigmoid(x):
    return 0.5 * jnp.tanh(0.5 * x) + 0.5


def _rms_scale(x, width):
    return lax.rsqrt(jnp.sum(x * x, axis=-1, keepdims=True) * (1.0 / width) + EPS)


def _adaln_kernel(c_ref, w_ref, b_ref, o_ref):
    c = c_ref[...]
    act = (c * _sigmoid(c)).astype(BF16)
    o_ref[...] = jnp.dot(act, w_ref[...].astype(BF16), preferred_element_type=F32) + b_ref[...]


def _adaln(c, w, b):
    bsz, d = c.shape
    n = w.shape[1]
    tn = 1024
    return pl.pallas_call(
        _adaln_kernel,
        out_shape=jax.ShapeDtypeStruct((bsz, n), F32),
        grid=(n // tn,),
        in_specs=[
            pl.BlockSpec((bsz, d), lambda j: (0, 0)),
            pl.BlockSpec((d, tn), lambda j: (0, j)),
            pl.BlockSpec((1, tn), lambda j: (0, j)),
        ],
        out_specs=pl.BlockSpec((bsz, tn), lambda j: (0, j)),
        compiler_params=pltpu.CompilerParams(
            dimension_semantics=("arbitrary",), vmem_limit_bytes=VMEM_LIMIT_BYTES),
        name="adaln",
    )(c, w, b.reshape(1, n))


def _proj_kernel(x_ref, mod_ref, g1_ref, wa_ref, wglu_ref, wgate_ref, gq_ref, wuq_ref, wuqs_ref,
                 gkv_ref, wuk_ref, wuvt_ref, aq_ref, bq_ref, ak_ref, bk_ref,
                 qt_ref, k_ref, vt_ref, u_ref, gate_ref):
    def half_tile(r):
        x = x_ref[0, r, :]
        scale_row = g1_ref[...] * (1.0 + mod_ref[0, 1:2, :])
        shift_row = mod_ref[0, 0:1, :]
        hb = (x * _rms_scale(x, x.shape[-1]) * scale_row + shift_row).astype(BF16)

        z_a = jnp.dot(hb, wa_ref[...], preferred_element_type=F32)

        zq = z_a[:, :Q_LORA_RANK]
        qn = (zq * _rms_scale(zq, Q_LORA_RANK) * gq_ref[...]).astype(BF16)
        contract_last = (((1,), (1,)), ((), ()))
        q_pre = lax.dot_general(wuq_ref[...], qn, contract_last, preferred_element_type=F32)
        q_swp = lax.dot_general(wuqs_ref[...], qn, contract_last, preferred_element_type=F32)
        aq = aq_ref[:, r]
        bq = bq_ref[:, r]
        for h in range(N_HEADS):
            sl = slice(h * HEAD_BLOCK, (h + 1) * HEAD_BLOCK)
            blk = q_pre[sl, :]
            ss = jnp.sum(blk * blk, axis=0, keepdims=True)
            qh = lax.rsqrt(ss * (1.0 / QK_HEAD_DIM) + EPS) * (blk * aq + q_swp[sl, :] * bq)
            qt_ref[0, sl, r] = qh.astype(BF16)

        zkv = z_a[:, Q_LORA_RANK:Q_LORA_RANK + KV_LORA_RANK]
        kvn = (zkv * _rms_scale(zkv, KV_LORA_RANK) * gkv_ref[...]).astype(BF16)
        k_up = jnp.dot(kvn, wuk_ref[...], preferred_element_type=F32)

        kr_blk = z_a[:, Q_LORA_RANK + KV_LORA_RANK:]
        lane = lax.broadcasted_iota(jnp.int32, (1, LANES), 1)
        kr_only = jnp.where(lane < QK_ROPE_DIM, kr_blk, 0.0)
        ss_kr = jnp.sum(kr_only * kr_only, axis=-1, keepdims=True)
        ak = ak_ref[r, :]
        bk = bk_ref[r, :]
        kr_rot = (pltpu.roll(kr_blk, QK_NOPE_DIM, 1) * ak
                  + pltpu.roll(kr_blk, QK_NOPE_DIM - QK_ROPE_DIM, 1) * bk)
        for h in range(N_HEADS):
            sl = slice(h * HEAD_BLOCK, (h + 1) * HEAD_BLOCK)
            blk = k_up[:, sl]
            ss = jnp.sum(blk * blk, axis=-1, keepdims=True) + ss_kr
            kh = lax.rsqrt(ss * (1.0 / QK_HEAD_DIM) + EPS) * (blk * ak + kr_rot)
            k_ref[0, r, sl] = kh.astype(BF16)

        vt = lax.dot_general(wuvt_ref[...], kvn, (((1,), (1,)), ((), ())), preferred_element_type=F32)
        row = lax.broadcasted_iota(jnp.int32, vt.shape, 0)
        vt_ref[0, :, r] = jnp.where(row % VT_ROWS == V_HEAD_DIM, 1.0, vt).astype(BF16)

        zg = jnp.dot(hb, wglu_ref[...], preferred_element_type=F32)
        u_ref[0, r, :] = zg[:, :CONV_CH] * _sigmoid(zg[:, CONV_CH:])

        gate_ref[0, r, :] = _sigmoid(
            jnp.dot(hb, wgate_ref[...], preferred_element_type=F32)).astype(BF16)

    t = x_ref.shape[1]
    for r in (slice(0, t // 2), slice(t // 2, t)):
        half_tile(r)


def _const_spec(shape):
    nd = len(shape)
    return pl.BlockSpec(shape, lambda *_: (0,) * nd, pipeline_mode=pl.Buffered(1))


def _proj(x, mod, g1, wa, wglu, wgate, gq, wuq, wuqs, gkv, wuk, wuvt, aq, bq, ak, bk):
    bsz, seq, d = x.shape
    t = PROJ_TILE
    d_qk = N_HEADS * HEAD_BLOCK
    vt_rows = N_HEADS * VT_ROWS
    tok = lambda w: pl.BlockSpec((1, t, w), lambda b, s: (b, s, 0))
    pos = lambda: pl.BlockSpec((t, LANES), lambda b, s: (s, 0))
    pos_t = lambda: pl.BlockSpec((LANES, t), lambda b, s: (0, s))
    by_rows = lambda n: pl.BlockSpec((1, n, t), lambda b, s: (b, 0, s))
    out_shapes = (
        jax.ShapeDtypeStruct((bsz, d_qk, seq), BF16),
        jax.ShapeDtypeStruct((bsz, seq, d_qk), BF16),
        jax.ShapeDtypeStruct((bsz, vt_rows, seq), BF16),
        jax.ShapeDtypeStruct((bsz, seq, CONV_CH), F32),
        jax.ShapeDtypeStruct((bsz, seq, 2 * d), BF16),
    )
    return pl.pallas_call(
        _proj_kernel,
        out_shape=out_shapes,
        grid=(bsz, seq // t),
        in_specs=[
            tok(d),
            pl.BlockSpec((1, ADA_CHUNKS, d), lambda b, s: (b, 0, 0)),
            _const_spec(g1.shape), _const_spec(wa.shape), _const_spec(wglu.shape),
            _const_spec(wgate.shape), _const_spec(gq.shape), _const_spec(wuq.shape),
            _const_spec(wuqs.shape), _const_spec(gkv.shape), _const_spec(wuk.shape),
            _const_spec(wuvt.shape), pos_t(), pos_t(), pos(), pos(),
        ],
        out_specs=(by_rows(d_qk), tok(d_qk), by_rows(vt_rows), tok(CONV_CH), tok(2 * d)),
        compiler_params=pltpu.CompilerParams(
            dimension_semantics=("arbitrary", "arbitrary"), vmem_limit_bytes=VMEM_LIMIT_BYTES),
        name="proj",
    )(x, mod, g1, wa, wglu, wgate, gq, wuq, wuqs, gkv, wuk, wuvt, aq, bq, ak, bk)


def _attn_kernel(qt_ref, k_ref, vt_ref, o_ref):
    seq = k_ref.shape[1]
    qb = ATTN_QBLOCK
    neg = jnp.finfo(F32).min
    key_chunk = lax.broadcasted_iota(jnp.int32, (qb, qb), 0) // CHUNK
    query_chunk = lax.broadcasted_iota(jnp.int32, (qb, qb), 1) // CHUNK
    diag_mask = key_chunk <= query_chunk
    blocks = [(j, hh) for j in range(seq // qb) for hh in range(2)]

    def key_chunks(j):
        q0 = j * qb
        edges = list(range(0, q0, ATTN_KCHUNK)) + [q0]
        return [(a, b, False) for a, b in zip(edges[:-1], edges[1:])] + [(q0, q0 + qb, True)]

    def score_steps(j, hh, out):
        sl = slice(hh * HEAD_BLOCK, (hh + 1) * HEAD_BLOCK)
        qt = qt_ref[0, sl, j * qb:(j + 1) * qb]
        m = None
        for a, b, diag in key_chunks(j):
            s = jnp.dot(k_ref[0, a:b, sl], qt, preferred_element_type=F32)
            if diag:
                s = jnp.where(diag_mask, s, neg)
            cm = jnp.max(s, axis=0, keepdims=True)
            m = cm if m is None else jnp.maximum(m, cm)
            out.append(s)
            yield
        out.append(m)

    def value_steps(j, hh, sc, out):
        rows = slice(hh * VT_ROWS, (hh + 1) * VT_ROWS)
        m = sc[-1]
        ot = None
        for (a, b, _), s in zip(key_chunks(j), sc[:-1]):
            part = jnp.dot(vt_ref[0, rows, a:b], jnp.exp2(s - m).astype(BF16),
                           preferred_element_type=F32)
            ot = part if ot is None else ot + part
            yield
        out.append(ot[:V_HEAD_DIM] / ot[V_HEAD_DIM:V_HEAD_DIM + 1])

    halves = []
    pending = []
    for _ in score_steps(*blocks[0], pending):
        pass
    for n, (j, hh) in enumerate(blocks):
        current, pending = pending, []
        nxt = score_steps(*blocks[n + 1], pending) if n + 1 < len(blocks) else iter(())
        cur = value_steps(j, hh, current, halves)
        live = [nxt, cur]
        while live:
            for it in list(live):
                if next(it, StopIteration) is StopIteration:
                    live.remove(it)
        if hh == 1:
            o_ref[0, j * qb:(j + 1) * qb, :] = jnp.concatenate(halves, axis=0).T.astype(BF16)
            halves = []


def _attn(qt, k, vt):
    bsz, seq, d_qk = k.shape
    pair = 2 * HEAD_BLOCK
    spec = pl.BlockSpec((1, seq, pair), lambda b, p: (b, 0, p))
    by_rows = lambda n: pl.BlockSpec((1, n, seq), lambda b, p: (b, p, 0))
    return pl.pallas_call(
        _attn_kernel,
        out_shape=jax.ShapeDtypeStruct((bsz, seq, N_HEADS * V_HEAD_DIM), BF16),
        grid=(bsz, d_qk // pair),
        in_specs=[by_rows(pair), spec, by_rows(2 * VT_ROWS)],
        out_specs=pl.BlockSpec((1, seq, 2 * V_HEAD_DIM), lambda b, p: (b, 0, p)),
        compiler_params=pltpu.CompilerParams(
            dimension_semantics=("arbitrary", "arbitrary"), vmem_limit_bytes=VMEM_LIMIT_BYTES),
        name="attn",
    )(qt, k, vt)


def _conv_group(*args, **kwargs):
    for _ in _conv_group_steps(*args, **kwargs):
        pass


def _conv_group_steps(g, u_ref, halo, out_ref, ubuf, wbc, cb_ref, lng_ref, lnb_ref,
                      before_chunk=None, after_chunk=None):
    if isinstance(g, int):
        g0 = g * CONV_GROUP
        history = halo if g == 0 else u_ref[g0 - HALO:g0, :]
    else:
        g0 = pl.multiple_of(g * CONV_GROUP, CONV_GROUP)
        prev0 = pl.multiple_of(jnp.maximum(g0 - HALO, 0), HALO)
        history = jnp.where(g == 0, halo, u_ref[pl.ds(prev0, HALO), :])
    ext = CONV_GROUP + HALO
    ubuf[0, 0:HALO, :] = history
    ubuf[0, HALO:, :] = u_ref[pl.ds(g0, CONV_GROUP), :]
    for p in range(1, SUBLANES):
        ubuf[p, 0:ext - SUBLANES, :] = ubuf[0, p:ext - SUBLANES + p, :]

    tap0 = HALO - (CONV_WIDTH - 1)
    blocks = CONV_ROWS // SUBLANES
    for c in range(CONV_GROUP // CONV_ROWS):
        r0 = c * CONV_ROWS
        bias = jnp.broadcast_to(cb_ref[...], (SUBLANES, CONV_CH))
        token = None if before_chunk is None else before_chunk(c)
        if token is not None:
            bias = bias + jnp.concatenate([_unknown_zero(token)] * (CONV_CH // LANES), axis=1)
        acc = [bias] * blocks
        for rb0 in range(0, blocks, CONV_ACC_BLOCKS):
            for tap in range(CONV_WIDTH):
                blk, phase = divmod(tap0 + tap, SUBLANES)
                w = wbc[tap]
                for rb in range(rb0, rb0 + CONV_ACC_BLOCKS):
                    row = r0 + (blk + rb) * SUBLANES
                    acc[rb] = acc[rb] + ubuf[phase, row:row + SUBLANES, :] * w
        conv = jnp.concatenate(acc, axis=0)
        cen = conv - jnp.mean(conv, axis=-1, keepdims=True)
        var = jnp.mean(cen * cen, axis=-1, keepdims=True)
        y = cen * lax.rsqrt(var + EPS) * lng_ref[...] + lnb_ref[...]
        out_ref[pl.ds(g0 + r0, CONV_ROWS), :] = (y * _sigmoid(y)).astype(BF16)
        if after_chunk is not None:
            after_chunk(c, y[0:SUBLANES, 0:LANES])
        yield


def _unknown_zero(tile):
    sixteen = jnp.uint32(16)
    bits = pltpu.bitcast(tile, jnp.uint32)
    zero_bits = lax.shift_right_logical(lax.shift_right_logical(bits, sixteen), sixteen)
    return pltpu.bitcast(zero_bits, F32)


def _order_after(ref, rows, tile):
    zero = _unknown_zero(tile)
    keep = ref[rows, 0:LANES].astype(F32) + jnp.concatenate([zero, zero], axis=0)
    ref[rows, 0:LANES] = keep.astype(BF16)


def _ffn_kernel(x_ref, attn_ref, gate_ref, mod_ref, ufirst_ref, unext_ref, halo_ref,
                wo_ref, cw_ref, cb_ref, lng_ref, lnb_ref, wpw_ref, wout_ref, g2_ref, w1_ref, w2_ref,
                o_ref, ubuf, wbc, cbuf, h2_ref, hid_ref, *, tiles_per_seq):
    t, d = x_ref.shape
    i = pl.program_id(0)
    slot = i % 2
    groups = t // CONV_GROUP
    conv_args = (ubuf, wbc, cb_ref, lng_ref, lnb_ref)

    @pl.when(i == 0)
    def _():
        for tap in range(CONV_WIDTH):
            wbc[tap] = jnp.broadcast_to(cw_ref[tap:tap + 1, :], (SUBLANES, CONV_CH))
        zero_halo = jnp.zeros((HALO, CONV_CH), F32)

        def first_tile(g, carry):
            _conv_group(g, ufirst_ref, zero_halo, cbuf.at[0], *conv_args)
            return carry

        lax.fori_loop(0, groups, first_tile, 0)

    half = t // 2
    row_halves = [slice(0, half), slice(half, t)]

    y_a = [jnp.dot(attn_ref[r, :], wo_ref[...], preferred_element_type=F32) for r in row_halves]
    y_b = [jnp.dot(cbuf[slot, r, :], wpw_ref[...], preferred_element_type=F32) for r in row_halves]
    mixed = []
    for r, ya, yb in zip(row_halves, y_a, y_b):
        mix = (gate_ref[r, :d].astype(F32) * ya + gate_ref[r, d:].astype(F32) * yb).astype(BF16)
        mixed.append(jnp.dot(mix, wout_ref[...], preferred_element_type=F32))

    next_starts_seq = (i + 1) % tiles_per_seq == 0
    halo_next = jnp.where(next_starts_seq, 0.0, halo_ref[...])

    pin_rows = [slice(r.start, r.start + 2 * SUBLANES) for r in row_halves]
    window_ends = list(itertools.accumulate(CONV_WINDOWS))
    window_pins = [(h2_ref, pin_rows[0]), (h2_ref, pin_rows[1]),
                   (hid_ref, pin_rows[0]), (hid_ref, pin_rows[1]), None]
    state = {"token": None, "done": 0}

    def before_chunk(c):
        return state["token"]

    def after_chunk(c, tile):
        state["done"] += 1
        if state["done"] in window_ends:
            pin = window_pins[window_ends.index(state["done"])]
            if pin is not None:
                _order_after(*pin, tile)

    def conv_steps():
        for g in range(groups):
            yield from _conv_group_steps(g, unext_ref, halo_next, cbuf.at[1 - slot], *conv_args,
                                         before_chunk=before_chunk, after_chunk=after_chunk)

    conv = conv_steps()

    def run_window(k, token):
        state["token"] = token
        while state["done"] < window_ends[k]:
            next(conv)

    def up_phase(r):
        for c in range(0, w1_ref.shape[1], FF_CHUNK):
            cols = slice(c, c + FF_CHUNK)
            up = jnp.maximum(jnp.dot(h2_ref[r, :], w1_ref[:, cols], preferred_element_type=F32), 0.0)
            hid_ref[r, cols] = (up * up).astype(BF16)
        return up[0:SUBLANES, 0:LANES]

    scale_row = g2_ref[...] * (1.0 + mod_ref[0, 4:5, :])
    for r, mx in zip(row_halves, mixed):
        x1 = x_ref[r, :] + mod_ref[0, 2:3, :] * mx
        o_ref[r, :] = x1
        h2_ref[r, :] = (x1 * _rms_scale(x1, d) * scale_row + mod_ref[0, 3:4, :]).astype(BF16)

    run_window(0, None)
    run_window(1, mixed[1][0:SUBLANES, 0:LANES])
    token = up_phase(row_halves[0])
    run_window(2, token)
    token = up_phase(row_halves[1])
    run_window(3, token)
    for n, r in enumerate(row_halves):
        down = jnp.dot(hid_ref[r, :], w2_ref[...], preferred_element_type=F32)
        o_ref[r, :] = o_ref[r, :] + mod_ref[0, 5:6, :] * down
        if n == 0:
            run_window(4, down[0:SUBLANES, 0:LANES])


def _ffn(x, attn, u, gates, mod, wo, cw, cb, lng, lnb, wpw, wout, g2, w1, w2):
    bsz, seq, d = x.shape
    t = FFN_TILE
    tiles_per_seq = seq // t
    n = bsz * tiles_per_seq
    rows = bsz * seq
    flat = lambda a: a.reshape(rows, a.shape[-1])
    tok = lambda w: pl.BlockSpec((t, w), lambda i: (i, 0))
    next_tile = lambda i: jnp.minimum(i + 1, n - 1)
    kern = functools.partial(_ffn_kernel, tiles_per_seq=tiles_per_seq)
    out = pl.pallas_call(
        kern,
        out_shape=jax.ShapeDtypeStruct((rows, d), x.dtype),
        grid=(n,),
        in_specs=[
            tok(d), tok(attn.shape[-1]), tok(2 * d),
            pl.BlockSpec((1, ADA_CHUNKS, d), lambda i: (i // tiles_per_seq, 0, 0)),
            pl.BlockSpec((t, CONV_CH), lambda i: (0, 0), pipeline_mode=pl.Buffered(1)),
            pl.BlockSpec((t, CONV_CH), lambda i: (next_tile(i), 0)),
            pl.BlockSpec((HALO, CONV_CH), lambda i: (next_tile(i) * (t // HALO) - 1, 0)),
            _const_spec(wo.shape), _const_spec(cw.shape), _const_spec(cb.shape),
            _const_spec(lng.shape), _const_spec(lnb.shape), _const_spec(wpw.shape),
            _const_spec(wout.shape), _const_spec(g2.shape), _const_spec(w1.shape),
            _const_spec(w2.shape),
        ],
        out_specs=tok(d),
        scratch_shapes=[
            pltpu.VMEM((SUBLANES, CONV_GROUP + HALO, CONV_CH), F32),
            pltpu.VMEM((CONV_WIDTH, SUBLANES, CONV_CH), F32),
            pltpu.VMEM((2, t, CONV_CH), BF16),
            pltpu.VMEM((t, d), BF16),
            pltpu.VMEM((t, w1.shape[1]), BF16),
        ],
        compiler_params=pltpu.CompilerParams(
            dimension_semantics=("arbitrary",), vmem_limit_bytes=VMEM_LIMIT_BYTES),
        name="ffn",
    )(flat(x), flat(attn), flat(gates), mod, flat(u), flat(u), flat(u),
      wo, cw, cb, lng, lnb, wpw, wout, g2, w1, w2)
    return out.reshape(bsz, seq, d)


def _head_blocks(w, width):
    return w.reshape(w.shape[0], N_HEADS, width)


def _pad_heads(w3):
    k, h, w = w3.shape
    return jnp.pad(w3, ((0, 0), (0, 0), (0, HEAD_BLOCK - w))).reshape(k, h * HEAD_BLOCK)


def _swap_rope_halves(w3):
    nope = jnp.zeros_like(w3[..., :QK_NOPE_DIM])
    lo = w3[..., QK_NOPE_DIM:QK_NOPE_DIM + ROPE_HALF]
    hi = w3[..., QK_NOPE_DIM + ROPE_HALF:]
    return jnp.concatenate([nope, hi, lo], axis=-1)


def _rope_gain_tables(seq, gain, premul):
    inv_freq = ROPE_THETA ** (-jnp.arange(0, QK_ROPE_DIM, 2, dtype=F32) / QK_ROPE_DIM)
    ang = jnp.arange(seq, dtype=F32)[:, None] * inv_freq[None, :]
    cos, sin = jnp.cos(ang), jnp.sin(ang)
    pad = HEAD_BLOCK - QK_HEAD_DIM
    g_nope = gain[:QK_NOPE_DIM]
    g_lo = gain[QK_NOPE_DIM:QK_NOPE_DIM + ROPE_HALF]
    g_hi = gain[QK_NOPE_DIM + ROPE_HALF:]
    a = jnp.concatenate([jnp.broadcast_to(g_nope, (seq, QK_NOPE_DIM)), cos * g_lo, cos * g_hi,
                         jnp.zeros((seq, pad), F32)], axis=-1)
    b = jnp.concatenate([jnp.zeros((seq, QK_NOPE_DIM), F32), -sin * g_hi, sin * g_lo,
                         jnp.zeros((seq, pad), F32)], axis=-1)
    return a * premul, b * premul


def kernel(x, c, w_ada, b_ada, norm1_g, w_in, q_latent_g, w_uq, kv_latent_g, w_ukv, qk_norm_q_g,
           qk_norm_k_g, w_o_mla, conv_w, conv_b, conv_ln_g, conv_ln_b, w_pw_out, w_out, norm2_g,
           w_ff1, w_ff2):
    bsz, seq, d = x.shape
    depth = w_ada.shape[0]
    off_q = Q_LORA_RANK
    off_kv = off_q + KV_LORA_RANK
    off_kr = off_kv + QK_ROPE_DIM
    off_glu = off_kr + 2 * CONV_CH
    row = lambda v: v.reshape(1, -1)

    q_premul = (QK_HEAD_DIM ** -0.5) * math.log2(math.e)

    for l in range(depth):
        wi = w_in[l]
        kr_cols = wi[:, off_kv:off_kr]
        kr_swapped = jnp.concatenate([kr_cols[:, ROPE_HALF:], kr_cols[:, :ROPE_HALF]], axis=-1)
        wa = jnp.concatenate(
            [wi[:, :off_kr], kr_swapped,
             jnp.zeros((d, 4 * LANES - off_kr - QK_ROPE_DIM), wi.dtype)], axis=-1).astype(BF16)
        wglu = wi[:, off_kr:off_glu].astype(BF16)
        wgate = wi[:, off_glu:].astype(BF16)

        uq3 = _head_blocks(w_uq[l], QK_HEAD_DIM)
        wuq = _pad_heads(uq3).T.astype(BF16)
        wuqs = _pad_heads(_swap_rope_halves(uq3)).T.astype(BF16)

        ukv3 = _head_blocks(w_ukv[l], QK_NOPE_DIM + V_HEAD_DIM)
        wuk = _pad_heads(ukv3[..., :QK_NOPE_DIM]).astype(BF16)
        wuvt = jnp.pad(ukv3[..., QK_NOPE_DIM:], ((0, 0), (0, 0), (0, VT_ROWS - V_HEAD_DIM)))
        wuvt = wuvt.reshape(KV_LORA_RANK, N_HEADS * VT_ROWS).T.astype(BF16)

        aq, bq = (tab.T for tab in _rope_gain_tables(seq, qk_norm_q_g[l], q_premul))
        ak, bk = _rope_gain_tables(seq, qk_norm_k_g[l], 1.0)

        mod = _adaln(c, w_ada[l], b_ada[l]).reshape(bsz, ADA_CHUNKS, d)
        qt, k, vt, u, gates = _proj(
            x, mod, row(norm1_g[l]), wa, wglu, wgate, row(q_latent_g[l]), wuq, wuqs,
            row(kv_latent_g[l]), wuk, wuvt, aq, bq, ak, bk)
        attn = _attn(qt, k, vt)
        x = _ffn(x, attn, u, gates, mod, w_o_mla[l].astype(BF16), conv_w[l], row(conv_b[l]),
                 row(conv_ln_g[l]), row(conv_ln_b[l]), w_pw_out[l].astype(BF16),
                 w_out[l].astype(BF16), row(norm2_g[l]), w_ff1[l].astype(BF16),
                 w_ff2[l].astype(BF16))
    return x
```

```python
import functools
import itertools
import math

import jax
import jax.numpy as jnp
from jax import lax
from jax.experimental import pallas as pl
from jax.experimental.pallas import tpu as pltpu

F32 = jnp.float32
BF16 = jnp.bfloat16

CHUNK = 64
N_HEADS = 8
QK_NOPE_DIM = 64
QK_ROPE_DIM = 32
QK_HEAD_DIM = QK_NOPE_DIM + QK_ROPE_DIM
V_HEAD_DIM = 64
Q_LORA_RANK = 256
KV_LORA_RANK = 128
CONV_CH = 512
CONV_WIDTH = 31
ADA_CHUNKS = 6
ROPE_THETA = 10000.0
EPS = 1e-6

LANES = 128
SUBLANES = 8
VMEM_LIMIT_BYTES = 56 * 1024 * 1024

HEAD_BLOCK = LANES
VT_ROWS = 80
HALO = 32
ROPE_HALF = QK_ROPE_DIM // 2

PROJ_TILE = 512
FFN_TILE = 512
ATTN_QBLOCK = 256
ATTN_KCHUNK = 1024
ATTN_LOOKAHEAD = 3
CONV_ROWS = 32
CONV_ACC_BLOCKS = 2
CONV_WINDOWS = (3, 4, 4, 4, 1)
CONV_GROUP = 128
FF_CHUNK = 1024


def _sigmoid(x):
    return 0.5 * jnp.tanh(0.5 * x) + 0.5


def _rms_scale(x, width):
    return lax.rsqrt(jnp.sum(x * x, axis=-1, keepdims=True) * (1.0 / width) + EPS)


def _adaln_kernel(c_ref, w_ref, b_ref, o_ref):
    c = c_ref[...]
    act = (c * _sigmoid(c)).astype(BF16)
    o_ref[...] = jnp.dot(act, w_ref[...].astype(BF16), preferred_element_type=F32) + b_ref[...]


def _adaln(c, w, b):
    bsz, d = c.shape
    n = w.shape[1]
    tn = 1024
    return pl.pallas_call(
        _adaln_kernel,
        out_shape=jax.ShapeDtypeStruct((bsz, n), F32),
        grid=(n // tn,),
        in_specs=[
            pl.BlockSpec((bsz, d), lambda j: (0, 0)),
            pl.BlockSpec((d, tn), lambda j: (0, j)),
            pl.BlockSpec((1, tn), lambda j: (0, j)),
        ],
        out_specs=pl.BlockSpec((bsz, tn), lambda j: (0, j)),
        compiler_params=pltpu.CompilerParams(
            dimension_semantics=("arbitrary",), vmem_limit_bytes=VMEM_LIMIT_BYTES),
        name="adaln",
    )(c, w, b.reshape(1, n))


def _proj_kernel(x_ref, mod_ref, g1_ref, wa_ref, wglu_ref, wgate_ref, gq_ref, wuq_ref, wuqs_ref,
                 gkv_ref, wuk_ref, wuvt_ref, aq_ref, bq_ref, ak_ref, bk_ref,
                 qt_ref, k_ref, vt_ref, u_ref, gate_ref):
    def half_tile(r):
        x = x_ref[0, r, :]
        scale_row = g1_ref[...] * (1.0 + mod_ref[0, 1:2, :])
        shift_row = mod_ref[0, 0:1, :]
        hb = (x * _rms_scale(x, x.shape[-1]) * scale_row + shift_row).astype(BF16)

        z_a = jnp.dot(hb, wa_ref[...], preferred_element_type=F32)

        zq = z_a[:, :Q_LORA_RANK]
        qn = (zq * _rms_scale(zq, Q_LORA_RANK) * gq_ref[...]).astype(BF16)
        contract_last = (((1,), (1,)), ((), ()))
        q_pre = lax.dot_general(wuq_ref[...], qn, contract_last, preferred_element_type=F32)
        q_swp = lax.dot_general(wuqs_ref[...], qn, contract_last, preferred_element_type=F32)
        aq = aq_ref[:, r]
        bq = bq_ref[:, r]
        for h in range(N_HEADS):
            sl = slice(h * HEAD_BLOCK, (h + 1) * HEAD_BLOCK)
            blk = q_pre[sl, :]
            ss = jnp.sum(blk * blk, axis=0, keepdims=True)
            qh = lax.rsqrt(ss * (1.0 / QK_HEAD_DIM) + EPS) * (blk * aq + q_swp[sl, :] * bq)
            qt_ref[0, sl, r] = qh.astype(BF16)

        zkv = z_a[:, Q_LORA_RANK:Q_LORA_RANK + KV_LORA_RANK]
        kvn = (zkv * _rms_scale(zkv, KV_LORA_RANK) * gkv_ref[...]).astype(BF16)
        k_up = jnp.dot(kvn, wuk_ref[...], preferred_element_type=F32)

        kr_blk = z_a[:, Q_LORA_RANK + KV_LORA_RANK:]
        lane = lax.broadcasted_iota(jnp.int32, (1, LANES), 1)
        kr_only = jnp.where(lane < QK_ROPE_DIM, kr_blk, 0.0)
        ss_kr = jnp.sum(kr_only * kr_only, axis=-1, keepdims=True)
        ak = ak_ref[r, :]
        bk = bk_ref[r, :]
        kr_rot = (pltpu.roll(kr_blk, QK_NOPE_DIM, 1) * ak
                  + pltpu.roll(kr_blk, QK_NOPE_DIM - QK_ROPE_DIM, 1) * bk)
        for h in range(N_HEADS):
            sl = slice(h * HEAD_BLOCK, (h + 1) * HEAD_BLOCK)
            blk = k_up[:, sl]
            ss = jnp.sum(blk * blk, axis=-1, keepdims=True) + ss_kr
            kh = lax.rsqrt(ss * (1.0 / QK_HEAD_DIM) + EPS) * (blk * ak + kr_rot)
            k_ref[0, r, sl] = kh.astype(BF16)

        vt = lax.dot_general(wuvt_ref[...], kvn, (((1,), (1,)), ((), ())), preferred_element_type=F32)
        row = lax.broadcasted_iota(jnp.int32, vt.shape, 0)
        vt_ref[0, :, r] = jnp.where(row % VT_ROWS == V_HEAD_DIM, 1.0, vt).astype(BF16)

        zg = jnp.dot(hb, wglu_ref[...], preferred_element_type=F32)
        u_ref[0, r, :] = zg[:, :CONV_CH] * _sigmoid(zg[:, CONV_CH:])

        gate_ref[0, r, :] = _sigmoid(
            jnp.dot(hb, wgate_ref[...], preferred_element_type=F32)).astype(BF16)

    t = x_ref.shape[1]
    for r in (slice(0, t // 2), slice(t // 2, t)):
        half_tile(r)


def _const_spec(shape):
    nd = len(shape)
    return pl.BlockSpec(shape, lambda *_: (0,) * nd, pipeline_mode=pl.Buffered(1))


def _proj(x, mod, g1, wa, wglu, wgate, gq, wuq, wuqs, gkv, wuk, wuvt, aq, bq, ak, bk):
    bsz, seq, d = x.shape
    t = PROJ_TILE
    d_qk = N_HEADS * HEAD_BLOCK
    vt_rows = N_HEADS * VT_ROWS
    tok = lambda w: pl.BlockSpec((1, t, w), lambda b, s: (b, s, 0))
    pos = lambda: pl.BlockSpec((t, LANES), lambda b, s: (s, 0))
    pos_t = lambda: pl.BlockSpec((LANES, t), lambda b, s: (0, s))
    by_rows = lambda n: pl.BlockSpec((1, n, t), lambda b, s: (b, 0, s))
    out_shapes = (
        jax.ShapeDtypeStruct((bsz, d_qk, seq), BF16),
        jax.ShapeDtypeStruct((bsz, seq, d_qk), BF16),
        jax.ShapeDtypeStruct((bsz, vt_rows, seq), BF16),
        jax.ShapeDtypeStruct((bsz, seq, CONV_CH), F32),
        jax.ShapeDtypeStruct((bsz, seq, 2 * d), BF16),
    )
    return pl.pallas_call(
        _proj_kernel,
        out_shape=out_shapes,
        grid=(bsz, seq // t),
        in_specs=[
            tok(d),
            pl.BlockSpec((1, ADA_CHUNKS, d), lambda b, s: (b, 0, 0)),
            _const_spec(g1.shape), _const_spec(wa.shape), _const_spec(wglu.shape),
            _const_spec(wgate.shape), _const_spec(gq.shape), _const_spec(wuq.shape),
            _const_spec(wuqs.shape), _const_spec(gkv.shape), _const_spec(wuk.shape),
            _const_spec(wuvt.shape), pos_t(), pos_t(), pos(), pos(),
        ],
        out_specs=(by_rows(d_qk), tok(d_qk), by_rows(vt_rows), tok(CONV_CH), tok(2 * d)),
        compiler_params=pltpu.CompilerParams(
            dimension_semantics=("arbitrary", "arbitrary"), vmem_limit_bytes=VMEM_LIMIT_BYTES),
        name="proj",
    )(x, mod, g1, wa, wglu, wgate, gq, wuq, wuqs, gkv, wuk, wuvt, aq, bq, ak, bk)


def _attn_kernel(qt_ref, k_ref, vt_ref, o_ref):
    seq = k_ref.shape[1]
    qb = ATTN_QBLOCK
    neg = jnp.finfo(F32).min
    key_chunk = lax.broadcasted_iota(jnp.int32, (qb, qb), 0) // CHUNK
    query_chunk = lax.broadcasted_iota(jnp.int32, (qb, qb), 1) // CHUNK
    diag_mask = key_chunk <= query_chunk
    blocks = [(j, hh) for j in range(seq // qb) for hh in range(2)]

    def key_chunks(j):
        q0 = j * qb
        edges = list(range(0, q0, ATTN_KCHUNK)) + [q0]
        return [(a, b, False) for a, b in zip(edges[:-1], edges[1:])] + [(q0, q0 + qb, True)]

    def score_steps(j, hh, out):
        sl = slice(hh * HEAD_BLOCK, (hh + 1) * HEAD_BLOCK)
        qt = qt_ref[0, sl, j * qb:(j + 1) * qb]
        m = None
        for a, b, diag in key_chunks(j):
            s = jnp.dot(k_ref[0, a:b, sl], qt, preferred_element_type=F32)
            if diag:
                s = jnp.where(diag_mask, s, neg)
            cm = jnp.max(s, axis=0, keepdims=True)
            m = cm if m is None else jnp.maximum(m, cm)
            out.append(s)
            yield
        out.append(m)

    def value_steps(j, hh, sc, out):
        rows = slice(hh * VT_ROWS, (hh + 1) * VT_ROWS)
        m = sc[-1]
        ot = None
        for (a, b, _), s in zip(key_chunks(j), sc[:-1]):
            part = jnp.dot(vt_ref[0, rows, a:b], jnp.exp2(s - m).astype(BF16),
                           preferred_element_type=F32)
            ot = part if ot is None else ot + part
            yield
        out.append(ot[:V_HEAD_DIM] / ot[V_HEAD_DIM:V_HEAD_DIM + 1])

    halves = []
    ahead = ATTN_LOOKAHEAD
    queued = []
    for n in range(min(ahead, len(blocks))):
        sc = []
        for _ in score_steps(*blocks[n], sc):
            pass
        queued.append(sc)
    for n, (j, hh) in enumerate(blocks):
        current = queued.pop(0)
        pending = []
        nxt = score_steps(*blocks[n + ahead], pending) if n + ahead < len(blocks) else iter(())
        cur = value_steps(j, hh, current, halves)
        live = [nxt, cur]
        while live:
            for it in list(live):
                if next(it, StopIteration) is StopIteration:
                    live.remove(it)
        if n + ahead < len(blocks):
            queued.append(pending)
        if hh == 1:
            o_ref[0, j * qb:(j + 1) * qb, :] = jnp.concatenate(halves, axis=0).T.astype(BF16)
            halves = []


def _attn(qt, k, vt):
    bsz, seq, d_qk = k.shape
    pair = 2 * HEAD_BLOCK
    spec = pl.BlockSpec((1, seq, pair), lambda b, p: (b, 0, p))
    by_rows = lambda n: pl.BlockSpec((1, n, seq), lambda b, p: (b, p, 0))
    return pl.pallas_call(
        _attn_kernel,
        out_shape=jax.ShapeDtypeStruct((bsz, seq, N_HEADS * V_HEAD_DIM), BF16),
        grid=(bsz, d_qk // pair),
        in_specs=[by_rows(pair), spec, by_rows(2 * VT_ROWS)],
        out_specs=pl.BlockSpec((1, seq, 2 * V_HEAD_DIM), lambda b, p: (b, 0, p)),
        compiler_params=pltpu.CompilerParams(
            dimension_semantics=("arbitrary", "arbitrary"), vmem_limit_bytes=VMEM_LIMIT_BYTES),
        name="attn",
    )(qt, k, vt)


def _conv_group(*args, **kwargs):
    for _ in _conv_group_steps(*args, **kwargs):
        pass


def _conv_group_steps(g, u_ref, halo, out_ref, ubuf, wbc, cb_ref, lng_ref, lnb_ref,
                      before_chunk=None, after_chunk=None):
    if isinstance(g, int):
        g0 = g * CONV_GROUP
        history = halo if g == 0 else u_ref[g0 - HALO:g0, :]
    else:
        g0 = pl.multiple_of(g * CONV_GROUP, CONV_GROUP)
        prev0 = pl.multiple_of(jnp.maximum(g0 - HALO, 0), HALO)
        history = jnp.where(g == 0, halo, u_ref[pl.ds(prev0, HALO), :])
    ext = CONV_GROUP + HALO
    ubuf[0, 0:HALO, :] = history
    ubuf[0, HALO:, :] = u_ref[pl.ds(g0, CONV_GROUP), :]
    for p in range(1, SUBLANES):
        ubuf[p, 0:ext - SUBLANES, :] = ubuf[0, p:ext - SUBLANES + p, :]

    tap0 = HALO - (CONV_WIDTH - 1)
    blocks = CONV_ROWS // SUBLANES
    for c in range(CONV_GROUP // CONV_ROWS):
        r0 = c * CONV_ROWS
        bias = jnp.broadcast_to(cb_ref[...], (SUBLANES, CONV_CH))
        token = None if before_chunk is None else before_chunk(c)
        if token is not None:
            bias = bias + jnp.concatenate([_unknown_zero(token)] * (CONV_CH // LANES), axis=1)
        acc = [bias] * blocks
        for rb0 in range(0, blocks, CONV_ACC_BLOCKS):
            for tap in range(CONV_WIDTH):
                blk, phase = divmod(tap0 + tap, SUBLANES)
                w = wbc[tap]
                for rb in range(rb0, rb0 + CONV_ACC_BLOCKS):
                    row = r0 + (blk + rb) * SUBLANES
                    acc[rb] = acc[rb] + ubuf[phase, row:row + SUBLANES, :] * w
        conv = jnp.concatenate(acc, axis=0)
        cen = conv - jnp.mean(conv, axis=-1, keepdims=True)
        var = jnp.mean(cen * cen, axis=-1, keepdims=True)
        y = cen * lax.rsqrt(var + EPS) * lng_ref[...] + lnb_ref[...]
        out_ref[pl.ds(g0 + r0, CONV_ROWS), :] = (y * _sigmoid(y)).astype(BF16)
        if after_chunk is not None:
            after_chunk(c, y[0:SUBLANES, 0:LANES])
        yield


def _unknown_zero(tile):
    sixteen = jnp.uint32(16)
    bits = pltpu.bitcast(tile, jnp.uint32)
    zero_bits = lax.shift_right_logical(lax.shift_right_logical(bits, sixteen), sixteen)
    return pltpu.bitcast(zero_bits, F32)


def _order_after(ref, rows, tile):
    zero = _unknown_zero(tile)
    keep = ref[rows, 0:LANES].astype(F32) + jnp.concatenate([zero, zero], axis=0)
    ref[rows, 0:LANES] = keep.astype(BF16)


def _ffn_kernel(x_ref, attn_ref, gate_ref, mod_ref, ufirst_ref, unext_ref, halo_ref,
                wo_ref, cw_ref, cb_ref, lng_ref, lnb_ref, wpw_ref, wout_ref, g2_ref, w1_ref, w2_ref,
                o_ref, ubuf, wbc, cbuf, h2_ref, hid_ref, *, tiles_per_seq):
    t, d = x_ref.shape
    i = pl.program_id(0)
    slot = i % 2
    groups = t // CONV_GROUP
    conv_args = (ubuf, wbc, cb_ref, lng_ref, lnb_ref)

    @pl.when(i == 0)
    def _():
        for tap in range(CONV_WIDTH):
            wbc[tap] = jnp.broadcast_to(cw_ref[tap:tap + 1, :], (SUBLANES, CONV_CH))
        zero_halo = jnp.zeros((HALO, CONV_CH), F32)

        def first_tile(g, carry):
            _conv_group(g, ufirst_ref, zero_halo, cbuf.at[0], *conv_args)
            return carry

        lax.fori_loop(0, groups, first_tile, 0)

    half = t // 2
    row_halves = [slice(0, half), slice(half, t)]

    y_a = [jnp.dot(attn_ref[r, :], wo_ref[...], preferred_element_type=F32) for r in row_halves]
    y_b = [jnp.dot(cbuf[slot, r, :], wpw_ref[...], preferred_element_type=F32) for r in row_halves]
    mixed = []
    for r, ya, yb in zip(row_halves, y_a, y_b):
        mix = (gate_ref[r, :d].astype(F32) * ya + gate_ref[r, d:].astype(F32) * yb).astype(BF16)
        mixed.append(jnp.dot(mix, wout_ref[...], preferred_element_type=F32))

    next_starts_seq = (i + 1) % tiles_per_seq == 0
    halo_next = jnp.where(next_starts_seq, 0.0, halo_ref[...])

    pin_rows = [slice(r.start, r.start + 2 * SUBLANES) for r in row_halves]
    window_ends = list(itertools.accumulate(CONV_WINDOWS))
    window_pins = [(h2_ref, pin_rows[0]), (h2_ref, pin_rows[1]),
                   (hid_ref, pin_rows[0]), (hid_ref, pin_rows[1]), None]
    state = {"token": None, "done": 0}

    def before_chunk(c):
        return state["token"]

    def after_chunk(c, tile):
        state["done"] += 1
        if state["done"] in window_ends:
            pin = window_pins[window_ends.index(state["done"])]
            if pin is not None:
                _order_after(*pin, tile)

    def conv_steps():
        for g in range(groups):
            yield from _conv_group_steps(g, unext_ref, halo_next, cbuf.at[1 - slot], *conv_args,
                                         before_chunk=before_chunk, after_chunk=after_chunk)

    conv = conv_steps()

    def run_window(k, token):
        state["token"] = token
        while state["done"] < window_ends[k]:
            next(conv)

    def up_phase(r):
        for c in range(0, w1_ref.shape[1], FF_CHUNK):
            cols = slice(c, c + FF_CHUNK)
            up = jnp.maximum(jnp.dot(h2_ref[r, :], w1_ref[:, cols], preferred_element_type=F32), 0.0)
            hid_ref[r, cols] = (up * up).astype(BF16)
        return up[0:SUBLANES, 0:LANES]

    scale_row = g2_ref[...] * (1.0 + mod_ref[0, 4:5, :])
    for r, mx in zip(row_halves, mixed):
        x1 = x_ref[r, :] + mod_ref[0, 2:3, :] * mx
        o_ref[r, :] = x1
        h2_ref[r, :] = (x1 * _rms_scale(x1, d) * scale_row + mod_ref[0, 3:4, :]).astype(BF16)

    run_window(0, None)
    run_window(1, mixed[1][0:SUBLANES, 0:LANES])
    token = up_phase(row_halves[0])
    run_window(2, token)
    token = up_phase(row_halves[1])
    run_window(3, token)
    for n, r in enumerate(row_halves):
        down = jnp.dot(hid_ref[r, :], w2_ref[...], preferred_element_type=F32)
        o_ref[r, :] = o_ref[r, :] + mod_ref[0, 5:6, :] * down
        if n == 0:
            run_window(4, down[0:SUBLANES, 0:LANES])


def _ffn(x, attn, u, gates, mod, wo, cw, cb, lng, lnb, wpw, wout, g2, w1, w2):
    bsz, seq, d = x.shape
    t = FFN_TILE
    tiles_per_seq = seq // t
    n = bsz * tiles_per_seq
    rows = bsz * seq
    flat = lambda a: a.reshape(rows, a.shape[-1])
    tok = lambda w: pl.BlockSpec((t, w), lambda i: (i, 0))
    next_tile = lambda i: jnp.minimum(i + 1, n - 1)
    kern = functools.partial(_ffn_kernel, tiles_per_seq=tiles_per_seq)
    out = pl.pallas_call(
        kern,
        out_shape=jax.ShapeDtypeStruct((rows, d), x.dtype),
        grid=(n,),
        in_specs=[
            tok(d), tok(attn.shape[-1]), tok(2 * d),
            pl.BlockSpec((1, ADA_CHUNKS, d), lambda i: (i // tiles_per_seq, 0, 0)),
            pl.BlockSpec((t, CONV_CH), lambda i: (0, 0), pipeline_mode=pl.Buffered(1)),
            pl.BlockSpec((t, CONV_CH), lambda i: (next_tile(i), 0)),
            pl.BlockSpec((HALO, CONV_CH), lambda i: (next_tile(i) * (t // HALO) - 1, 0)),
            _const_spec(wo.shape), _const_spec(cw.shape), _const_spec(cb.shape),
            _const_spec(lng.shape), _const_spec(lnb.shape), _const_spec(wpw.shape),
            _const_spec(wout.shape), _const_spec(g2.shape), _const_spec(w1.shape),
            _const_spec(w2.shape),
        ],
        out_specs=tok(d),
        scratch_shapes=[
            pltpu.VMEM((SUBLANES, CONV_GROUP + HALO, CONV_CH), F32),
            pltpu.VMEM((CONV_WIDTH, SUBLANES, CONV_CH), F32),
            pltpu.VMEM((2, t, CONV_CH), BF16),
            pltpu.VMEM((t, d), BF16),
            pltpu.VMEM((t, w1.shape[1]), BF16),
        ],
        compiler_params=pltpu.CompilerParams(
            dimension_semantics=("arbitrary",), vmem_limit_bytes=VMEM_LIMIT_BYTES),
        name="ffn",
    )(flat(x), flat(attn), flat(gates), mod, flat(u), flat(u), flat(u),
      wo, cw, cb, lng, lnb, wpw, wout, g2, w1, w2)
    return out.reshape(bsz, seq, d)


def _head_blocks(w, width):
    return w.reshape(w.shape[0], N_HEADS, width)


def _pad_heads(w3):
    k, h, w = w3.shape
    return jnp.pad(w3, ((0, 0), (0, 0), (0, HEAD_BLOCK - w))).reshape(k, h * HEAD_BLOCK)


def _swap_rope_halves(w3):
    nope = jnp.zeros_like(w3[..., :QK_NOPE_DIM])
    lo = w3[..., QK_NOPE_DIM:QK_NOPE_DIM + ROPE_HALF]
    hi = w3[..., QK_NOPE_DIM + ROPE_HALF:]
    return jnp.concatenate([nope, hi, lo], axis=-1)


def _rope_gain_tables(seq, gain, premul):
    inv_freq = ROPE_THETA ** (-jnp.arange(0, QK_ROPE_DIM, 2, dtype=F32) / QK_ROPE_DIM)
    ang = jnp.arange(seq, dtype=F32)[:, None] * inv_freq[None, :]
    cos, sin = jnp.cos(ang), jnp.sin(ang)
    pad = HEAD_BLOCK - QK_HEAD_DIM
    g_nope = gain[:QK_NOPE_DIM]
    g_lo = gain[QK_NOPE_DIM:QK_NOPE_DIM + ROPE_HALF]
    g_hi = gain[QK_NOPE_DIM + ROPE_HALF:]
    a = jnp.concatenate([jnp.broadcast_to(g_nope, (seq, QK_NOPE_DIM)), cos * g_lo, cos * g_hi,
                         jnp.zeros((seq, pad), F32)], axis=-1)
    b = jnp.concatenate([jnp.zeros((seq, QK_NOPE_DIM), F32), -sin * g_hi, sin * g_lo,
                         jnp.zeros((seq, pad), F32)], axis=-1)
    return a * premul, b * premul


def kernel(x, c, w_ada, b_ada, norm1_g, w_in, q_latent_g, w_uq, kv_latent_g, w_ukv, qk_norm_q_g,
           qk_norm_k_g, w_o_mla, conv_w, conv_b, conv_ln_g, conv_ln_b, w_pw_out, w_out, norm2_g,
           w_ff1, w_ff2):
    bsz, seq, d = x.shape
    depth = w_ada.shape[0]
    off_q = Q_LORA_RANK
    off_kv = off_q + KV_LORA_RANK
    off_kr = off_kv + QK_ROPE_DIM
    off_glu = off_kr + 2 * CONV_CH
    row = lambda v: v.reshape(1, -1)

    q_premul = (QK_HEAD_DIM ** -0.5) * math.log2(math.e)

    for l in range(depth):
        wi = w_in[l]
        kr_cols = wi[:, off_kv:off_kr]
        kr_swapped = jnp.concatenate([kr_cols[:, ROPE_HALF:], kr_cols[:, :ROPE_HALF]], axis=-1)
        wa = jnp.concatenate(
            [wi[:, :off_kr], kr_swapped,
             jnp.zeros((d, 4 * LANES - off_kr - QK_ROPE_DIM), wi.dtype)], axis=-1).astype(BF16)
        wglu = wi[:, off_kr:off_glu].astype(BF16)
        wgate = wi[:, off_glu:].astype(BF16)

        uq3 = _head_blocks(w_uq[l], QK_HEAD_DIM)
        wuq = _pad_heads(uq3).T.astype(BF16)
        wuqs = _pad_heads(_swap_rope_halves(uq3)).T.astype(BF16)

        ukv3 = _head_blocks(w_ukv[l], QK_NOPE_DIM + V_HEAD_DIM)
        wuk = _pad_heads(ukv3[..., :QK_NOPE_DIM]).astype(BF16)
        wuvt = jnp.pad(ukv3[..., QK_NOPE_DIM:], ((0, 0), (0, 0), (0, VT_ROWS - V_HEAD_DIM)))
        wuvt = wuvt.reshape(KV_LORA_RANK, N_HEADS * VT_ROWS).T.astype(BF16)

        aq, bq = (tab.T for tab in _rope_gain_tables(seq, qk_norm_q_g[l], q_premul))
        ak, bk = _rope_gain_tables(seq, qk_norm_k_g[l], 1.0)

        mod = _adaln(c, w_ada[l], b_ada[l]).reshape(bsz, ADA_CHUNKS, d)
        qt, k, vt, u, gates = _proj(
            x, mod, row(norm1_g[l]), wa, wglu, wgate, row(q_latent_g[l]), wuq, wuqs,
            row(kv_latent_g[l]), wuk, wuvt, aq, bq, ak, bk)
        attn = _attn(qt, k, vt)
        x = _ffn(x, attn, u, gates, mod, w_o_mla[l].astype(BF16), conv_w[l], row(conv_b[l]),
                 row(conv_ln_g[l]), row(conv_ln_b[l]), w_pw_out[l].astype(BF16),
                 w_out[l].astype(BF16), row(norm2_g[l]), w_ff1[l].astype(BF16),
                 w_ff2[l].astype(BF16))
    return x
```

```python
import functools
import itertools
import math

import jax
import jax.numpy as jnp
from jax import lax
from jax.experimental import pallas as pl
from jax.experimental.pallas import tpu as pltpu

F32 = jnp.float32
BF16 = jnp.bfloat16

CHUNK = 64
N_HEADS = 8
QK_NOPE_DIM = 64
QK_ROPE_DIM = 32
QK_HEAD_DIM = QK_NOPE_DIM + QK_ROPE_DIM
V_HEAD_DIM = 64
Q_LORA_RANK = 256
KV_LORA_RANK = 128
CONV_CH = 512
CONV_WIDTH = 31
ADA_CHUNKS = 6
ROPE_THETA = 10000.0
EPS = 1e-6

LANES = 128
SUBLANES = 8
VMEM_LIMIT_BYTES = 56 * 1024 * 1024

HEAD_BLOCK = LANES
VT_ROWS = 80
HALO = 32
ROPE_HALF = QK_ROPE_DIM // 2

PROJ_TILE = 512
FFN_TILE = 512
ATTN_QBLOCK = 256
ATTN_KCHUNK = 1024
ATTN_LOOKAHEAD = 3
CONV_ROWS = 32
CONV_ACC_BLOCKS = 2
CONV_WINDOWS = (3, 4, 4, 4, 1)
CONV_GROUP = 128
FF_CHUNK = 1024


def _sigmoid(x):
    return 0.5 * jnp.tanh(0.5 * x) + 0.5


def _rms_scale(x, width):
    return lax.rsqrt(jnp.sum(x * x, axis=-1, keepdims=True) * (1.0 / width) + EPS)


def _adaln_kernel(c_ref, w_ref, b_ref, o_ref):
    c = c_ref[...]
    act = (c * _sigmoid(c)).astype(BF16)
    o_ref[...] = jnp.dot(act, w_ref[...].astype(BF16), preferred_element_type=F32) + b_ref[...]


def _adaln(c, w, b):
    bsz, d = c.shape
    n = w.shape[1]
    tn = 1024
    return pl.pallas_call(
        _adaln_kernel,
        out_shape=jax.ShapeDtypeStruct((bsz, n), F32),
        grid=(n // tn,),
        in_specs=[
            pl.BlockSpec((bsz, d), lambda j: (0, 0)),
            pl.BlockSpec((d, tn), lambda j: (0, j)),
            pl.BlockSpec((1, tn), lambda j: (0, j)),
        ],
        out_specs=pl.BlockSpec((bsz, tn), lambda j: (0, j)),
        compiler_params=pltpu.CompilerParams(
            dimension_semantics=("arbitrary",), vmem_limit_bytes=VMEM_LIMIT_BYTES),
        name="adaln",
    )(c, w, b.reshape(1, n))


def _proj_kernel(x_ref, mod_ref, g1_ref, wa_ref, wglu_ref, wgate_ref, gq_ref, wuq_ref, wuqs_ref,
                 gkv_ref, wuk_ref, wuvt_ref, aq_ref, bq_ref, ak_ref, bk_ref,
                 qt_ref, k_ref, vt_ref, u_ref, gate_ref):
    def latents(r):
        x = x_ref[0, r, :]
        scale_row = g1_ref[...] * (1.0 + mod_ref[0, 1:2, :])
        shift_row = mod_ref[0, 0:1, :]
        hb = (x * _rms_scale(x, x.shape[-1]) * scale_row + shift_row).astype(BF16)
        return hb, jnp.dot(hb, wa_ref[...], preferred_element_type=F32)

    def glu(r, hb):
        zg = jnp.dot(hb, wglu_ref[...], preferred_element_type=F32)
        u_ref[0, r, :] = zg[:, :CONV_CH] * _sigmoid(zg[:, CONV_CH:])

    def gates(r, hb):
        gate_ref[0, r, :] = _sigmoid(
            jnp.dot(hb, wgate_ref[...], preferred_element_type=F32)).astype(BF16)

    def heads(r, z_a):
        zq = z_a[:, :Q_LORA_RANK]
        qn = (zq * _rms_scale(zq, Q_LORA_RANK) * gq_ref[...]).astype(BF16)
        contract_last = (((1,), (1,)), ((), ()))
        q_pre = lax.dot_general(wuq_ref[...], qn, contract_last, preferred_element_type=F32)
        q_swp = lax.dot_general(wuqs_ref[...], qn, contract_last, preferred_element_type=F32)
        aq = aq_ref[:, r]
        bq = bq_ref[:, r]
        for h in range(N_HEADS):
            sl = slice(h * HEAD_BLOCK, (h + 1) * HEAD_BLOCK)
            blk = q_pre[sl, :]
            ss = jnp.sum(blk * blk, axis=0, keepdims=True)
            qh = lax.rsqrt(ss * (1.0 / QK_HEAD_DIM) + EPS) * (blk * aq + q_swp[sl, :] * bq)
            qt_ref[0, sl, r] = qh.astype(BF16)

        zkv = z_a[:, Q_LORA_RANK:Q_LORA_RANK + KV_LORA_RANK]
        kvn = (zkv * _rms_scale(zkv, KV_LORA_RANK) * gkv_ref[...]).astype(BF16)
        k_up = jnp.dot(kvn, wuk_ref[...], preferred_element_type=F32)

        kr_blk = z_a[:, Q_LORA_RANK + KV_LORA_RANK:]
        lane = lax.broadcasted_iota(jnp.int32, (1, LANES), 1)
        kr_only = jnp.where(lane < QK_ROPE_DIM, kr_blk, 0.0)
        ss_kr = jnp.sum(kr_only * kr_only, axis=-1, keepdims=True)
        ak = ak_ref[r, :]
        bk = bk_ref[r, :]
        kr_rot = (pltpu.roll(kr_blk, QK_NOPE_DIM, 1) * ak
                  + pltpu.roll(kr_blk, QK_NOPE_DIM - QK_ROPE_DIM, 1) * bk)
        for h in range(N_HEADS):
            sl = slice(h * HEAD_BLOCK, (h + 1) * HEAD_BLOCK)
            blk = k_up[:, sl]
            ss = jnp.sum(blk * blk, axis=-1, keepdims=True) + ss_kr
            kh = lax.rsqrt(ss * (1.0 / QK_HEAD_DIM) + EPS) * (blk * ak + kr_rot)
            k_ref[0, r, sl] = kh.astype(BF16)

        vt = lax.dot_general(wuvt_ref[...], kvn, contract_last, preferred_element_type=F32)
        row = lax.broadcasted_iota(jnp.int32, vt.shape, 0)
        vt_ref[0, :, r] = jnp.where(row % VT_ROWS == V_HEAD_DIM, 1.0, vt).astype(BF16)

    t = x_ref.shape[1]
    ra, rb = slice(0, t // 2), slice(t // 2, t)
    hb_a, z_a = latents(ra)
    glu(ra, hb_a)
    hb_b, z_b = latents(rb)
    gates(ra, hb_a)
    heads(ra, z_a)
    glu(rb, hb_b)
    heads(rb, z_b)
    gates(rb, hb_b)


def _const_spec(shape):
    nd = len(shape)
    return pl.BlockSpec(shape, lambda *_: (0,) * nd, pipeline_mode=pl.Buffered(1))


def _proj(x, mod, g1, wa, wglu, wgate, gq, wuq, wuqs, gkv, wuk, wuvt, aq, bq, ak, bk):
    bsz, seq, d = x.shape
    t = PROJ_TILE
    d_qk = N_HEADS * HEAD_BLOCK
    vt_rows = N_HEADS * VT_ROWS
    tok = lambda w: pl.BlockSpec((1, t, w), lambda b, s: (b, s, 0))
    pos = lambda: pl.BlockSpec((t, LANES), lambda b, s: (s, 0))
    pos_t = lambda: pl.BlockSpec((LANES, t), lambda b, s: (0, s))
    by_rows = lambda n: pl.BlockSpec((1, n, t), lambda b, s: (b, 0, s))
    out_shapes = (
        jax.ShapeDtypeStruct((bsz, d_qk, seq), BF16),
        jax.ShapeDtypeStruct((bsz, seq, d_qk), BF16),
        jax.ShapeDtypeStruct((bsz, vt_rows, seq), BF16),
        jax.ShapeDtypeStruct((bsz, seq, CONV_CH), F32),
        jax.ShapeDtypeStruct((bsz, seq, 2 * d), BF16),
    )
    return pl.pallas_call(
        _proj_kernel,
        out_shape=out_shapes,
        grid=(bsz, seq // t),
        in_specs=[
            tok(d),
            pl.BlockSpec((1, ADA_CHUNKS, d), lambda b, s: (b, 0, 0)),
            _const_spec(g1.shape), _const_spec(wa.shape), _const_spec(wglu.shape),
            _const_spec(wgate.shape), _const_spec(gq.shape), _const_spec(wuq.shape),
            _const_spec(wuqs.shape), _const_spec(gkv.shape), _const_spec(wuk.shape),
            _const_spec(wuvt.shape), pos_t(), pos_t(), pos(), pos(),
        ],
        out_specs=(by_rows(d_qk), tok(d_qk), by_rows(vt_rows), tok(CONV_CH), tok(2 * d)),
        compiler_params=pltpu.CompilerParams(
            dimension_semantics=("arbitrary", "arbitrary"), vmem_limit_bytes=VMEM_LIMIT_BYTES),
        name="proj",
    )(x, mod, g1, wa, wglu, wgate, gq, wuq, wuqs, gkv, wuk, wuvt, aq, bq, ak, bk)


def _attn_kernel(qt_ref, k_ref, vt_ref, o_ref):
    seq = k_ref.shape[1]
    qb = ATTN_QBLOCK
    neg = jnp.finfo(F32).min
    key_chunk = lax.broadcasted_iota(jnp.int32, (qb, qb), 0) // CHUNK
    query_chunk = lax.broadcasted_iota(jnp.int32, (qb, qb), 1) // CHUNK
    diag_mask = key_chunk <= query_chunk
    blocks = [(j, hh) for j in range(seq // qb) for hh in range(2)]

    def key_chunks(j):
        q0 = j * qb
        edges = list(range(0, q0, ATTN_KCHUNK)) + [q0]
        return [(a, b, False) for a, b in zip(edges[:-1], edges[1:])] + [(q0, q0 + qb, True)]

    def score_steps(j, hh, out):
        sl = slice(hh * HEAD_BLOCK, (hh + 1) * HEAD_BLOCK)
        qt = qt_ref[0, sl, j * qb:(j + 1) * qb]
        m = None
        for a, b, diag in key_chunks(j):
            s = jnp.dot(k_ref[0, a:b, sl], qt, preferred_element_type=F32)
            if diag:
                s = jnp.where(diag_mask, s, neg)
            cm = jnp.max(s, axis=0, keepdims=True)
            m = cm if m is None else jnp.maximum(m, cm)
            out.append(s)
            yield
        out.append(m)

    def value_steps(j, hh, sc, out):
        rows = slice(hh * VT_ROWS, (hh + 1) * VT_ROWS)
        m = sc[-1]
        ot = None
        for (a, b, _), s in zip(key_chunks(j), sc[:-1]):
            part = jnp.dot(vt_ref[0, rows, a:b], jnp.exp2(s - m).astype(BF16),
                           preferred_element_type=F32)
            ot = part if ot is None else ot + part
            yield
        out.append(ot[:V_HEAD_DIM] / ot[V_HEAD_DIM:V_HEAD_DIM + 1])

    halves = []
    ahead = ATTN_LOOKAHEAD
    queued = []
    for n in range(min(ahead, len(blocks))):
        sc = []
        for _ in score_steps(*blocks[n], sc):
            pass
        queued.append(sc)
    for n, (j, hh) in enumerate(blocks):
        current = queued.pop(0)
        pending = []
        nxt = score_steps(*blocks[n + ahead], pending) if n + ahead < len(blocks) else iter(())
        cur = value_steps(j, hh, current, halves)
        live = [nxt, cur]
        while live:
            for it in list(live):
                if next(it, StopIteration) is StopIteration:
                    live.remove(it)
        if n + ahead < len(blocks):
            queued.append(pending)
        if hh == 1:
            o_ref[0, j * qb:(j + 1) * qb, :] = jnp.concatenate(halves, axis=0).T.astype(BF16)
            halves = []


def _attn(qt, k, vt):
    bsz, seq, d_qk = k.shape
    pair = 2 * HEAD_BLOCK
    spec = pl.BlockSpec((1, seq, pair), lambda b, p: (b, 0, p))
    by_rows = lambda n: pl.BlockSpec((1, n, seq), lambda b, p: (b, p, 0))
    return pl.pallas_call(
        _attn_kernel,
        out_shape=jax.ShapeDtypeStruct((bsz, seq, N_HEADS * V_HEAD_DIM), BF16),
        grid=(bsz, d_qk // pair),
        in_specs=[by_rows(pair), spec, by_rows(2 * VT_ROWS)],
        out_specs=pl.BlockSpec((1, seq, 2 * V_HEAD_DIM), lambda b, p: (b, 0, p)),
        compiler_params=pltpu.CompilerParams(
            dimension_semantics=("arbitrary", "arbitrary"), vmem_limit_bytes=VMEM_LIMIT_BYTES),
        name="attn",
    )(qt, k, vt)


def _conv_group(*args, **kwargs):
    for _ in _conv_group_steps(*args, **kwargs):
        pass


def _conv_group_steps(g, u_ref, halo, out_ref, ubuf, wbc, cb_ref, lng_ref, lnb_ref,
                      before_chunk=None, after_chunk=None):
    if isinstance(g, int):
        g0 = g * CONV_GROUP
        history = halo if g == 0 else u_ref[g0 - HALO:g0, :]
    else:
        g0 = pl.multiple_of(g * CONV_GROUP, CONV_GROUP)
        prev0 = pl.multiple_of(jnp.maximum(g0 - HALO, 0), HALO)
        history = jnp.where(g == 0, halo, u_ref[pl.ds(prev0, HALO), :])
    ext = CONV_GROUP + HALO
    ubuf[0, 0:HALO, :] = history
    ubuf[0, HALO:, :] = u_ref[pl.ds(g0, CONV_GROUP), :]
    for p in range(1, SUBLANES):
        ubuf[p, 0:ext - SUBLANES, :] = ubuf[0, p:ext - SUBLANES + p, :]

    tap0 = HALO - (CONV_WIDTH - 1)
    blocks = CONV_ROWS // SUBLANES
    for c in range(CONV_GROUP // CONV_ROWS):
        r0 = c * CONV_ROWS
        bias = jnp.broadcast_to(cb_ref[...], (SUBLANES, CONV_CH))
        token = None if before_chunk is None else before_chunk(c)
        if token is not None:
            bias = bias + jnp.concatenate([_unknown_zero(token)] * (CONV_CH // LANES), axis=1)
        acc = [bias] * blocks
        for rb0 in range(0, blocks, CONV_ACC_BLOCKS):
            for tap in range(CONV_WIDTH):
                blk, phase = divmod(tap0 + tap, SUBLANES)
                w = wbc[tap]
                for rb in range(rb0, rb0 + CONV_ACC_BLOCKS):
                    row = r0 + (blk + rb) * SUBLANES
                    acc[rb] = acc[rb] + ubuf[phase, row:row + SUBLANES, :] * w
        conv = jnp.concatenate(acc, axis=0)
        cen = conv - jnp.mean(conv, axis=-1, keepdims=True)
        var = jnp.mean(cen * cen, axis=-1, keepdims=True)
        y = cen * lax.rsqrt(var + EPS) * lng_ref[...] + lnb_ref[...]
        out_ref[pl.ds(g0 + r0, CONV_ROWS), :] = (y * _sigmoid(y)).astype(BF16)
        if after_chunk is not None:
            after_chunk(c, y[0:SUBLANES, 0:LANES])
        yield


def _unknown_zero(tile):
    sixteen = jnp.uint32(16)
    bits = pltpu.bitcast(tile, jnp.uint32)
    zero_bits = lax.shift_right_logical(lax.shift_right_logical(bits, sixteen), sixteen)
    return pltpu.bitcast(zero_bits, F32)


def _order_after(ref, rows, tile):
    zero = _unknown_zero(tile)
    keep = ref[rows, 0:LANES].astype(F32) + jnp.concatenate([zero, zero], axis=0)
    ref[rows, 0:LANES] = keep.astype(BF16)


def _ffn_kernel(x_ref, attn_ref, gate_ref, mod_ref, ufirst_ref, unext_ref, halo_ref,
                wo_ref, cw_ref, cb_ref, lng_ref, lnb_ref, wpw_ref, wout_ref, g2_ref, w1_ref, w2_ref,
                o_ref, ubuf, wbc, cbuf, h2_ref, hid_ref, *, tiles_per_seq):
    t, d = x_ref.shape
    i = pl.program_id(0)
    slot = i % 2
    groups = t // CONV_GROUP
    conv_args = (ubuf, wbc, cb_ref, lng_ref, lnb_ref)

    @pl.when(i == 0)
    def _():
        for tap in range(CONV_WIDTH):
            wbc[tap] = jnp.broadcast_to(cw_ref[tap:tap + 1, :], (SUBLANES, CONV_CH))
        zero_halo = jnp.zeros((HALO, CONV_CH), F32)

        def first_tile(g, carry):
            _conv_group(g, ufirst_ref, zero_halo, cbuf.at[0], *conv_args)
            return carry

        lax.fori_loop(0, groups, first_tile, 0)

    half = t // 2
    row_halves = [slice(0, half), slice(half, t)]

    y_a = [jnp.dot(attn_ref[r, :], wo_ref[...], preferred_element_type=F32) for r in row_halves]
    y_b = [jnp.dot(cbuf[slot, r, :], wpw_ref[...], preferred_element_type=F32) for r in row_halves]
    mixed = []
    for r, ya, yb in zip(row_halves, y_a, y_b):
        mix = (gate_ref[r, :d].astype(F32) * ya + gate_ref[r, d:].astype(F32) * yb).astype(BF16)
        mixed.append(jnp.dot(mix, wout_ref[...], preferred_element_type=F32))

    next_starts_seq = (i + 1) % tiles_per_seq == 0
    halo_next = jnp.where(next_starts_seq, 0.0, halo_ref[...])

    pin_rows = [slice(r.start, r.start + 2 * SUBLANES) for r in row_halves]
    window_ends = list(itertools.accumulate(CONV_WINDOWS))
    window_pins = [(h2_ref, pin_rows[0]), (h2_ref, pin_rows[1]),
                   (hid_ref, pin_rows[0]), (hid_ref, pin_rows[1]), None]
    state = {"token": None, "done": 0}

    def before_chunk(c):
        return state["token"]

    def after_chunk(c, tile):
        state["done"] += 1
        if state["done"] in window_ends:
            pin = window_pins[window_ends.index(state["done"])]
            if pin is not None:
                _order_after(*pin, tile)

    def conv_steps():
        for g in range(groups):
            yield from _conv_group_steps(g, unext_ref, halo_next, cbuf.at[1 - slot], *conv_args,
                                         before_chunk=before_chunk, after_chunk=after_chunk)

    conv = conv_steps()

    def run_window(k, token):
        state["token"] = token
        while state["done"] < window_ends[k]:
            next(conv)

    def up_phase(r):
        for c in range(0, w1_ref.shape[1], FF_CHUNK):
            cols = slice(c, c + FF_CHUNK)
            up = jnp.maximum(jnp.dot(h2_ref[r, :], w1_ref[:, cols], preferred_element_type=F32), 0.0)
            hid_ref[r, cols] = (up * up).astype(BF16)
        return up[0:SUBLANES, 0:LANES]

    scale_row = g2_ref[...] * (1.0 + mod_ref[0, 4:5, :])
    for r, mx in zip(row_halves, mixed):
        x1 = x_ref[r, :] + mod_ref[0, 2:3, :] * mx
        o_ref[r, :] = x1
        h2_ref[r, :] = (x1 * _rms_scale(x1, d) * scale_row + mod_ref[0, 3:4, :]).astype(BF16)

    run_window(0, None)
    run_window(1, mixed[1][0:SUBLANES, 0:LANES])
    token = up_phase(row_halves[0])
    run_window(2, token)
    token = up_phase(row_halves[1])
    run_window(3, token)
    for n, r in enumerate(row_halves):
        down = jnp.dot(hid_ref[r, :], w2_ref[...], preferred_element_type=F32)
        o_ref[r, :] = o_ref[r, :] + mod_ref[0, 5:6, :] * down
        if n == 0:
            run_window(4, down[0:SUBLANES, 0:LANES])


def _ffn(x, attn, u, gates, mod, wo, cw, cb, lng, lnb, wpw, wout, g2, w1, w2):
    bsz, seq, d = x.shape
    t = FFN_TILE
    tiles_per_seq = seq // t
    n = bsz * tiles_per_seq
    rows = bsz * seq
    flat = lambda a: a.reshape(rows, a.shape[-1])
    tok = lambda w: pl.BlockSpec((t, w), lambda i: (i, 0))
    next_tile = lambda i: jnp.minimum(i + 1, n - 1)
    kern = functools.partial(_ffn_kernel, tiles_per_seq=tiles_per_seq)
    out = pl.pallas_call(
        kern,
        out_shape=jax.ShapeDtypeStruct((rows, d), x.dtype),
        grid=(n,),
        in_specs=[
            tok(d), tok(attn.shape[-1]), tok(2 * d),
            pl.BlockSpec((1, ADA_CHUNKS, d), lambda i: (i // tiles_per_seq, 0, 0)),
            pl.BlockSpec((t, CONV_CH), lambda i: (0, 0), pipeline_mode=pl.Buffered(1)),
            pl.BlockSpec((t, CONV_CH), lambda i: (next_tile(i), 0)),
            pl.BlockSpec((HALO, CONV_CH), lambda i: (next_tile(i) * (t // HALO) - 1, 0)),
            _const_spec(wo.shape), _const_spec(cw.shape), _const_spec(cb.shape),
            _const_spec(lng.shape), _const_spec(lnb.shape), _const_spec(wpw.shape),
            _const_spec(wout.shape), _const_spec(g2.shape), _const_spec(w1.shape),
            _const_spec(w2.shape),
        ],
        out_specs=tok(d),
        scratch_shapes=[
            pltpu.VMEM((SUBLANES, CONV_GROUP + HALO, CONV_CH), F32),
            pltpu.VMEM((CONV_WIDTH, SUBLANES, CONV_CH), F32),
            pltpu.VMEM((2, t, CONV_CH), BF16),
            pltpu.VMEM((t, d), BF16),
            pltpu.VMEM((t, w1.shape[1]), BF16),
        ],
        compiler_params=pltpu.CompilerParams(
            dimension_semantics=("arbitrary",), vmem_limit_bytes=VMEM_LIMIT_BYTES),
        name="ffn",
    )(flat(x), flat(attn), flat(gates), mod, flat(u), flat(u), flat(u),
      wo, cw, cb, lng, lnb, wpw, wout, g2, w1, w2)
    return out.reshape(bsz, seq, d)


def _head_blocks(w, width):
    return w.reshape(w.shape[0], N_HEADS, width)


def _pad_heads(w3):
    k, h, w = w3.shape
    return jnp.pad(w3, ((0, 0), (0, 0), (0, HEAD_BLOCK - w))).reshape(k, h * HEAD_BLOCK)


def _swap_rope_halves(w3):
    nope = jnp.zeros_like(w3[..., :QK_NOPE_DIM])
    lo = w3[..., QK_NOPE_DIM:QK_NOPE_DIM + ROPE_HALF]
    hi = w3[..., QK_NOPE_DIM + ROPE_HALF:]
    return jnp.concatenate([nope, hi, lo], axis=-1)


def _rope_gain_tables(seq, gain, premul):
    inv_freq = ROPE_THETA ** (-jnp.arange(0, QK_ROPE_DIM, 2, dtype=F32) / QK_ROPE_DIM)
    ang = jnp.arange(seq, dtype=F32)[:, None] * inv_freq[None, :]
    cos, sin = jnp.cos(ang), jnp.sin(ang)
    pad = HEAD_BLOCK - QK_HEAD_DIM
    g_nope = gain[:QK_NOPE_DIM]
    g_lo = gain[QK_NOPE_DIM:QK_NOPE_DIM + ROPE_HALF]
    g_hi = gain[QK_NOPE_DIM + ROPE_HALF:]
    a = jnp.concatenate([jnp.broadcast_to(g_nope, (seq, QK_NOPE_DIM)), cos * g_lo, cos * g_hi,
                         jnp.zeros((seq, pad), F32)], axis=-1)
    b = jnp.concatenate([jnp.zeros((seq, QK_NOPE_DIM), F32), -sin * g_hi, sin * g_lo,
                         jnp.zeros((seq, pad), F32)], axis=-1)
    return a * premul, b * premul


def kernel(x, c, w_ada, b_ada, norm1_g, w_in, q_latent_g, w_uq, kv_latent_g, w_ukv, qk_norm_q_g,
           qk_norm_k_g, w_o_mla, conv_w, conv_b, conv_ln_g, conv_ln_b, w_pw_out, w_out, norm2_g,
           w_ff1, w_ff2):
    bsz, seq, d = x.shape
    depth = w_ada.shape[0]
    off_q = Q_LORA_RANK
    off_kv = off_q + KV_LORA_RANK
    off_kr = off_kv + QK_ROPE_DIM
    off_glu = off_kr + 2 * CONV_CH
    row = lambda v: v.reshape(1, -1)

    q_premul = (QK_HEAD_DIM ** -0.5) * math.log2(math.e)

    for l in range(depth):
        wi = w_in[l]
        kr_cols = wi[:, off_kv:off_kr]
        kr_swapped = jnp.concatenate([kr_cols[:, ROPE_HALF:], kr_cols[:, :ROPE_HALF]], axis=-1)
        wa = jnp.concatenate(
            [wi[:, :off_kr], kr_swapped,
             jnp.zeros((d, 4 * LANES - off_kr - QK_ROPE_DIM), wi.dtype)], axis=-1).astype(BF16)
        wglu = wi[:, off_kr:off_glu].astype(BF16)
        wgate = wi[:, off_glu:].astype(BF16)

        uq3 = _head_blocks(w_uq[l], QK_HEAD_DIM)
        wuq = _pad_heads(uq3).T.astype(BF16)
        wuqs = _pad_heads(_swap_rope_halves(uq3)).T.astype(BF16)

        ukv3 = _head_blocks(w_ukv[l], QK_NOPE_DIM + V_HEAD_DIM)
        wuk = _pad_heads(ukv3[..., :QK_NOPE_DIM]).astype(BF16)
        wuvt = jnp.pad(ukv3[..., QK_NOPE_DIM:], ((0, 0), (0, 0), (0, VT_ROWS - V_HEAD_DIM)))
        wuvt = wuvt.reshape(KV_LORA_RANK, N_HEADS * VT_ROWS).T.astype(BF16)

        aq, bq = (tab.T for tab in _rope_gain_tables(seq, qk_norm_q_g[l], q_premul))
        ak, bk = _rope_gain_tables(seq, qk_norm_k_g[l], 1.0)

        mod = _adaln(c, w_ada[l], b_ada[l]).reshape(bsz, ADA_CHUNKS, d)
        qt, k, vt, u, gates = _proj(
            x, mod, row(norm1_g[l]), wa, wglu, wgate, row(q_latent_g[l]), wuq, wuqs,
            row(kv_latent_g[l]), wuk, wuvt, aq, bq, ak, bk)
        attn = _attn(qt, k, vt)
        x = _ffn(x, attn, u, gates, mod, w_o_mla[l].astype(BF16), conv_w[l], row(conv_b[l]),
                 row(conv_ln_g[l]), row(conv_ln_b[l]), w_pw_out[l].astype(BF16),
                 w_out[l].astype(BF16), row(norm2_g[l]), w_ff1[l].astype(BF16),
                 w_ff2[l].astype(BF16))
    return x
```

```python
import functools
import itertools
import math

import jax
import jax.numpy as jnp
from jax import lax
from jax.experimental import pallas as pl
from jax.experimental.pallas import tpu as pltpu

F32 = jnp.float32
BF16 = jnp.bfloat16

CHUNK = 64
N_HEADS = 8
QK_NOPE_DIM = 64
QK_ROPE_DIM = 32
QK_HEAD_DIM = QK_NOPE_DIM + QK_ROPE_DIM
V_HEAD_DIM = 64
Q_LORA_RANK = 256
KV_LORA_RANK = 128
CONV_CH = 512
CONV_WIDTH = 31
ADA_CHUNKS = 6
ROPE_THETA = 10000.0
EPS = 1e-6

LANES = 128
SUBLANES = 8
VMEM_LIMIT_BYTES = 56 * 1024 * 1024

HEAD_BLOCK = LANES
VT_ROWS = 80
HALO = 32
ROPE_HALF = QK_ROPE_DIM // 2

PROJ_TILE = 512
FFN_TILE = 512
ATTN_QBLOCK = 256
ATTN_KCHUNK = 1024
ATTN_HEADS = 4
ATTN_LOOKAHEAD = 3
CONV_ROWS = 32
CONV_ACC_BLOCKS = 2
CONV_WINDOWS = (3, 4, 4, 4, 1)
CONV_GROUP = 128
FF_CHUNK = 1024


def _sigmoid(x):
    return 0.5 * jnp.tanh(0.5 * x) + 0.5


def _rms_scale(x, width):
    return lax.rsqrt(jnp.sum(x * x, axis=-1, keepdims=True) * (1.0 / width) + EPS)


def _adaln_kernel(c_ref, w_ref, b_ref, o_ref):
    c = c_ref[...]
    act = (c * _sigmoid(c)).astype(BF16)
    o_ref[...] = jnp.dot(act, w_ref[...].astype(BF16), preferred_element_type=F32) + b_ref[...]


def _adaln(c, w, b):
    bsz, d = c.shape
    n = w.shape[1]
    tn = 1024
    return pl.pallas_call(
        _adaln_kernel,
        out_shape=jax.ShapeDtypeStruct((bsz, n), F32),
        grid=(n // tn,),
        in_specs=[
            pl.BlockSpec((bsz, d), lambda j: (0, 0)),
            pl.BlockSpec((d, tn), lambda j: (0, j)),
            pl.BlockSpec((1, tn), lambda j: (0, j)),
        ],
        out_specs=pl.BlockSpec((bsz, tn), lambda j: (0, j)),
        compiler_params=pltpu.CompilerParams(
            dimension_semantics=("arbitrary",), vmem_limit_bytes=VMEM_LIMIT_BYTES),
        name="adaln",
    )(c, w, b.reshape(1, n))


def _proj_kernel(x_ref, mod_ref, g1_ref, wa_ref, wglu_ref, wgate_ref, gq_ref, wuq_ref, wuqs_ref,
                 gkv_ref, wuk_ref, wuvt_ref, aq_ref, bq_ref, ak_ref, bk_ref,
                 qt_ref, k_ref, vt_ref, u_ref, gate_ref):
    def latents(r):
        x = x_ref[0, r, :]
        scale_row = g1_ref[...] * (1.0 + mod_ref[0, 1:2, :])
        shift_row = mod_ref[0, 0:1, :]
        hb = (x * _rms_scale(x, x.shape[-1]) * scale_row + shift_row).astype(BF16)
        return hb, jnp.dot(hb, wa_ref[...], preferred_element_type=F32)

    def glu(r, hb):
        zg = jnp.dot(hb, wglu_ref[...], preferred_element_type=F32)
        u_ref[0, r, :] = zg[:, :CONV_CH] * _sigmoid(zg[:, CONV_CH:])

    def gates(r, hb):
        gate_ref[0, r, :] = _sigmoid(
            jnp.dot(hb, wgate_ref[...], preferred_element_type=F32)).astype(BF16)

    def heads(r, z_a):
        zq = z_a[:, :Q_LORA_RANK]
        qn = (zq * _rms_scale(zq, Q_LORA_RANK) * gq_ref[...]).astype(BF16)
        contract_last = (((1,), (1,)), ((), ()))
        q_pre = lax.dot_general(wuq_ref[...], qn, contract_last, preferred_element_type=F32)
        q_swp = lax.dot_general(wuqs_ref[...], qn, contract_last, preferred_element_type=F32)
        aq = aq_ref[:, r]
        bq = bq_ref[:, r]
        for h in range(N_HEADS):
            sl = slice(h * HEAD_BLOCK, (h + 1) * HEAD_BLOCK)
            blk = q_pre[sl, :]
            ss = jnp.sum(blk * blk, axis=0, keepdims=True)
            qh = lax.rsqrt(ss * (1.0 / QK_HEAD_DIM) + EPS) * (blk * aq + q_swp[sl, :] * bq)
            qt_ref[0, sl, r] = qh.astype(BF16)

        zkv = z_a[:, Q_LORA_RANK:Q_LORA_RANK + KV_LORA_RANK]
        kvn = (zkv * _rms_scale(zkv, KV_LORA_RANK) * gkv_ref[...]).astype(BF16)
        k_up = jnp.dot(kvn, wuk_ref[...], preferred_element_type=F32)

        kr_blk = z_a[:, Q_LORA_RANK + KV_LORA_RANK:]
        lane = lax.broadcasted_iota(jnp.int32, (1, LANES), 1)
        kr_only = jnp.where(lane < QK_ROPE_DIM, kr_blk, 0.0)
        ss_kr = jnp.sum(kr_only * kr_only, axis=-1, keepdims=True)
        ak = ak_ref[r, :]
        bk = bk_ref[r, :]
        kr_rot = (pltpu.roll(kr_blk, QK_NOPE_DIM, 1) * ak
                  + pltpu.roll(kr_blk, QK_NOPE_DIM - QK_ROPE_DIM, 1) * bk)
        for h in range(N_HEADS):
            sl = slice(h * HEAD_BLOCK, (h + 1) * HEAD_BLOCK)
            blk = k_up[:, sl]
            ss = jnp.sum(blk * blk, axis=-1, keepdims=True) + ss_kr
            kh = lax.rsqrt(ss * (1.0 / QK_HEAD_DIM) + EPS) * (blk * ak + kr_rot)
            k_ref[0, r, sl] = kh.astype(BF16)

        vt = lax.dot_general(wuvt_ref[...], kvn, contract_last, preferred_element_type=F32)
        row = lax.broadcasted_iota(jnp.int32, vt.shape, 0)
        vt_ref[0, :, r] = jnp.where(row % VT_ROWS == V_HEAD_DIM, 1.0, vt).astype(BF16)

    t = x_ref.shape[1]
    ra, rb = slice(0, t // 2), slice(t // 2, t)
    hb_a, z_a = latents(ra)
    glu(ra, hb_a)
    hb_b, z_b = latents(rb)
    gates(ra, hb_a)
    heads(ra, z_a)
    glu(rb, hb_b)
    heads(rb, z_b)
    gates(rb, hb_b)


def _const_spec(shape):
    nd = len(shape)
    return pl.BlockSpec(shape, lambda *_: (0,) * nd, pipeline_mode=pl.Buffered(1))


def _proj(x, mod, g1, wa, wglu, wgate, gq, wuq, wuqs, gkv, wuk, wuvt, aq, bq, ak, bk):
    bsz, seq, d = x.shape
    t = PROJ_TILE
    d_qk = N_HEADS * HEAD_BLOCK
    vt_rows = N_HEADS * VT_ROWS
    tok = lambda w: pl.BlockSpec((1, t, w), lambda b, s: (b, s, 0))
    pos = lambda: pl.BlockSpec((t, LANES), lambda b, s: (s, 0))
    pos_t = lambda: pl.BlockSpec((LANES, t), lambda b, s: (0, s))
    by_rows = lambda n: pl.BlockSpec((1, n, t), lambda b, s: (b, 0, s))
    out_shapes = (
        jax.ShapeDtypeStruct((bsz, d_qk, seq), BF16),
        jax.ShapeDtypeStruct((bsz, seq, d_qk), BF16),
        jax.ShapeDtypeStruct((bsz, vt_rows, seq), BF16),
        jax.ShapeDtypeStruct((bsz, seq, CONV_CH), F32),
        jax.ShapeDtypeStruct((bsz, seq, 2 * d), BF16),
    )
    return pl.pallas_call(
        _proj_kernel,
        out_shape=out_shapes,
        grid=(bsz, seq // t),
        in_specs=[
            tok(d),
            pl.BlockSpec((1, ADA_CHUNKS, d), lambda b, s: (b, 0, 0)),
            _const_spec(g1.shape), _const_spec(wa.shape), _const_spec(wglu.shape),
            _const_spec(wgate.shape), _const_spec(gq.shape), _const_spec(wuq.shape),
            _const_spec(wuqs.shape), _const_spec(gkv.shape), _const_spec(wuk.shape),
            _const_spec(wuvt.shape), pos_t(), pos_t(), pos(), pos(),
        ],
        out_specs=(by_rows(d_qk), tok(d_qk), by_rows(vt_rows), tok(CONV_CH), tok(2 * d)),
        compiler_params=pltpu.CompilerParams(
            dimension_semantics=("arbitrary", "arbitrary"), vmem_limit_bytes=VMEM_LIMIT_BYTES),
        name="proj",
    )(x, mod, g1, wa, wglu, wgate, gq, wuq, wuqs, gkv, wuk, wuvt, aq, bq, ak, bk)


def _attn_kernel(qt_ref, k_ref, vt_ref, o_ref):
    seq = k_ref.shape[1]
    qb = ATTN_QBLOCK
    neg = jnp.finfo(F32).min
    key_chunk = lax.broadcasted_iota(jnp.int32, (qb, qb), 0) // CHUNK
    query_chunk = lax.broadcasted_iota(jnp.int32, (qb, qb), 1) // CHUNK
    diag_mask = key_chunk <= query_chunk
    heads = o_ref.shape[2] // V_HEAD_DIM
    blocks = [(j, hh) for j in range(seq // qb) for hh in range(heads)]

    def key_chunks(j):
        q0 = j * qb
        edges = list(range(0, q0, ATTN_KCHUNK)) + [q0]
        return [(a, b, False) for a, b in zip(edges[:-1], edges[1:])] + [(q0, q0 + qb, True)]

    def score_steps(j, hh, out):
        sl = slice(hh * HEAD_BLOCK, (hh + 1) * HEAD_BLOCK)
        qt = qt_ref[0, sl, j * qb:(j + 1) * qb]
        m = None
        for a, b, diag in key_chunks(j):
            s = jnp.dot(k_ref[0, a:b, sl], qt, preferred_element_type=F32)
            if diag:
                s = jnp.where(diag_mask, s, neg)
            cm = jnp.max(s, axis=0, keepdims=True)
            m = cm if m is None else jnp.maximum(m, cm)
            out.append(s)
            yield
        out.append(m)

    def value_steps(j, hh, sc, out):
        rows = slice(hh * VT_ROWS, (hh + 1) * VT_ROWS)
        m = sc[-1]
        ot = None
        for (a, b, _), s in zip(key_chunks(j), sc[:-1]):
            part = jnp.dot(vt_ref[0, rows, a:b], jnp.exp2(s - m).astype(BF16),
                           preferred_element_type=F32)
            ot = part if ot is None else ot + part
            yield
        out.append(ot[:V_HEAD_DIM] / ot[V_HEAD_DIM:V_HEAD_DIM + 1])

    halves = []
    ahead = ATTN_LOOKAHEAD
    queued = []
    for n in range(min(ahead, len(blocks))):
        sc = []
        for _ in score_steps(*blocks[n], sc):
            pass
        queued.append(sc)
    for n, (j, hh) in enumerate(blocks):
        current = queued.pop(0)
        pending = []
        nxt = score_steps(*blocks[n + ahead], pending) if n + ahead < len(blocks) else iter(())
        cur = value_steps(j, hh, current, halves)
        live = [nxt, cur]
        while live:
            for it in list(live):
                if next(it, StopIteration) is StopIteration:
                    live.remove(it)
        if n + ahead < len(blocks):
            queued.append(pending)
        if hh == heads - 1:
            o_ref[0, j * qb:(j + 1) * qb, :] = jnp.concatenate(halves, axis=0).T.astype(BF16)
            halves = []


def _attn(qt, k, vt):
    bsz, seq, d_qk = k.shape
    group = ATTN_HEADS * HEAD_BLOCK
    spec = pl.BlockSpec((1, seq, group), lambda b, p: (b, 0, p))
    by_rows = lambda n: pl.BlockSpec((1, n, seq), lambda b, p: (b, p, 0))
    return pl.pallas_call(
        _attn_kernel,
        out_shape=jax.ShapeDtypeStruct((bsz, seq, N_HEADS * V_HEAD_DIM), BF16),
        grid=(bsz, d_qk // group),
        in_specs=[by_rows(group), spec, by_rows(ATTN_HEADS * VT_ROWS)],
        out_specs=pl.BlockSpec((1, seq, ATTN_HEADS * V_HEAD_DIM), lambda b, p: (b, 0, p)),
        compiler_params=pltpu.CompilerParams(
            dimension_semantics=("arbitrary", "arbitrary"), vmem_limit_bytes=VMEM_LIMIT_BYTES),
        name="attn",
    )(qt, k, vt)


def _conv_group(*args, **kwargs):
    for _ in _conv_group_steps(*args, **kwargs):
        pass


def _conv_group_steps(g, u_ref, halo, out_ref, ubuf, wbc, cb_ref, lng_ref, lnb_ref,
                      before_chunk=None, after_chunk=None):
    if isinstance(g, int):
        g0 = g * CONV_GROUP
        history = halo if g == 0 else u_ref[g0 - HALO:g0, :]
    else:
        g0 = pl.multiple_of(g * CONV_GROUP, CONV_GROUP)
        prev0 = pl.multiple_of(jnp.maximum(g0 - HALO, 0), HALO)
        history = jnp.where(g == 0, halo, u_ref[pl.ds(prev0, HALO), :])
    ext = CONV_GROUP + HALO
    ubuf[0, 0:HALO, :] = history
    ubuf[0, HALO:, :] = u_ref[pl.ds(g0, CONV_GROUP), :]
    for p in range(1, SUBLANES):
        ubuf[p, 0:ext - SUBLANES, :] = ubuf[0, p:ext - SUBLANES + p, :]

    tap0 = HALO - (CONV_WIDTH - 1)
    blocks = CONV_ROWS // SUBLANES
    for c in range(CONV_GROUP // CONV_ROWS):
        r0 = c * CONV_ROWS
        bias = jnp.broadcast_to(cb_ref[...], (SUBLANES, CONV_CH))
        token = None if before_chunk is None else before_chunk(c)
        if token is not None:
            bias = bias + jnp.concatenate([_unknown_zero(token)] * (CONV_CH // LANES), axis=1)
        acc = [bias] * blocks
        for rb0 in range(0, blocks, CONV_ACC_BLOCKS):
            for tap in range(CONV_WIDTH):
                blk, phase = divmod(tap0 + tap, SUBLANES)
                w = wbc[tap]
                for rb in range(rb0, rb0 + CONV_ACC_BLOCKS):
                    row = r0 + (blk + rb) * SUBLANES
                    acc[rb] = acc[rb] + ubuf[phase, row:row + SUBLANES, :] * w
        conv = jnp.concatenate(acc, axis=0)
        cen = conv - jnp.mean(conv, axis=-1, keepdims=True)
        var = jnp.mean(cen * cen, axis=-1, keepdims=True)
        y = cen * lax.rsqrt(var + EPS) * lng_ref[...] + lnb_ref[...]
        out_ref[pl.ds(g0 + r0, CONV_ROWS), :] = (y * _sigmoid(y)).astype(BF16)
        if after_chunk is not None:
            after_chunk(c, y[0:SUBLANES, 0:LANES])
        yield


def _unknown_zero(tile):
    sixteen = jnp.uint32(16)
    bits = pltpu.bitcast(tile, jnp.uint32)
    zero_bits = lax.shift_right_logical(lax.shift_right_logical(bits, sixteen), sixteen)
    return pltpu.bitcast(zero_bits, F32)


def _order_after(ref, rows, tile):
    zero = _unknown_zero(tile)
    keep = ref[rows, 0:LANES].astype(F32) + jnp.concatenate([zero, zero], axis=0)
    ref[rows, 0:LANES] = keep.astype(BF16)


def _ffn_kernel(x_ref, attn_ref, gate_ref, mod_ref, ufirst_ref, unext_ref, halo_ref,
                wo_ref, cw_ref, cb_ref, lng_ref, lnb_ref, wpw_ref, wout_ref, g2_ref, w1_ref, w2_ref,
                o_ref, ubuf, wbc, cbuf, h2_ref, hid_ref, *, tiles_per_seq):
    t, d = x_ref.shape
    i = pl.program_id(0)
    slot = i % 2
    groups = t // CONV_GROUP
    conv_args = (ubuf, wbc, cb_ref, lng_ref, lnb_ref)

    @pl.when(i == 0)
    def _():
        for tap in range(CONV_WIDTH):
            wbc[tap] = jnp.broadcast_to(cw_ref[tap:tap + 1, :], (SUBLANES, CONV_CH))
        zero_halo = jnp.zeros((HALO, CONV_CH), F32)

        def first_tile(g, carry):
            _conv_group(g, ufirst_ref, zero_halo, cbuf.at[0], *conv_args)
            return carry

        lax.fori_loop(0, groups, first_tile, 0)

    half = t // 2
    row_halves = [slice(0, half), slice(half, t)]

    y_a = [jnp.dot(attn_ref[r, :], wo_ref[...], preferred_element_type=F32) for r in row_halves]
    y_b = [jnp.dot(cbuf[slot, r, :], wpw_ref[...], preferred_element_type=F32) for r in row_halves]
    mixed = []
    for r, ya, yb in zip(row_halves, y_a, y_b):
        mix = (gate_ref[r, :d].astype(F32) * ya + gate_ref[r, d:].astype(F32) * yb).astype(BF16)
        mixed.append(jnp.dot(mix, wout_ref[...], preferred_element_type=F32))

    next_starts_seq = (i + 1) % tiles_per_seq == 0
    halo_next = jnp.where(next_starts_seq, 0.0, halo_ref[...])

    pin_rows = [slice(r.start, r.start + 2 * SUBLANES) for r in row_halves]
    window_ends = list(itertools.accumulate(CONV_WINDOWS))
    window_pins = [(h2_ref, pin_rows[0]), (h2_ref, pin_rows[1]),
                   (hid_ref, pin_rows[0]), (hid_ref, pin_rows[1]), None]
    state = {"token": None, "done": 0}

    def before_chunk(c):
        return state["token"]

    def after_chunk(c, tile):
        state["done"] += 1
        if state["done"] in window_ends:
            pin = window_pins[window_ends.index(state["done"])]
            if pin is not None:
                _order_after(*pin, tile)

    def conv_steps():
        for g in range(groups):
            yield from _conv_group_steps(g, unext_ref, halo_next, cbuf.at[1 - slot], *conv_args,
                                         before_chunk=before_chunk, after_chunk=after_chunk)

    conv = conv_steps()

    def run_window(k, token):
        state["token"] = token
        while state["done"] < window_ends[k]:
            next(conv)

    def up_phase(r):
        for c in range(0, w1_ref.shape[1], FF_CHUNK):
            cols = slice(c, c + FF_CHUNK)
            up = jnp.maximum(jnp.dot(h2_ref[r, :], w1_ref[:, cols], preferred_element_type=F32), 0.0)
            hid_ref[r, cols] = (up * up).astype(BF16)
        return up[0:SUBLANES, 0:LANES]

    scale_row = g2_ref[...] * (1.0 + mod_ref[0, 4:5, :])
    for r, mx in zip(row_halves, mixed):
        x1 = x_ref[r, :] + mod_ref[0, 2:3, :] * mx
        o_ref[r, :] = x1
        h2_ref[r, :] = (x1 * _rms_scale(x1, d) * scale_row + mod_ref[0, 3:4, :]).astype(BF16)

    run_window(0, None)
    run_window(1, mixed[1][0:SUBLANES, 0:LANES])
    token = up_phase(row_halves[0])
    run_window(2, token)
    token = up_phase(row_halves[1])
    run_window(3, token)
    for n, r in enumerate(row_halves):
        down = jnp.dot(hid_ref[r, :], w2_ref[...], preferred_element_type=F32)
        o_ref[r, :] = o_ref[r, :] + mod_ref[0, 5:6, :] * down
        if n == 0:
            run_window(4, down[0:SUBLANES, 0:LANES])


def _ffn(x, attn, u, gates, mod, wo, cw, cb, lng, lnb, wpw, wout, g2, w1, w2):
    bsz, seq, d = x.shape
    t = FFN_TILE
    tiles_per_seq = seq // t
    n = bsz * tiles_per_seq
    rows = bsz * seq
    flat = lambda a: a.reshape(rows, a.shape[-1])
    tok = lambda w: pl.BlockSpec((t, w), lambda i: (i, 0))
    next_tile = lambda i: jnp.minimum(i + 1, n - 1)
    kern = functools.partial(_ffn_kernel, tiles_per_seq=tiles_per_seq)
    out = pl.pallas_call(
        kern,
        out_shape=jax.ShapeDtypeStruct((rows, d), x.dtype),
        grid=(n,),
        in_specs=[
            tok(d), tok(attn.shape[-1]), tok(2 * d),
            pl.BlockSpec((1, ADA_CHUNKS, d), lambda i: (i // tiles_per_seq, 0, 0)),
            pl.BlockSpec((t, CONV_CH), lambda i: (0, 0), pipeline_mode=pl.Buffered(1)),
            pl.BlockSpec((t, CONV_CH), lambda i: (next_tile(i), 0)),
            pl.BlockSpec((HALO, CONV_CH), lambda i: (next_tile(i) * (t // HALO) - 1, 0)),
            _const_spec(wo.shape), _const_spec(cw.shape), _const_spec(cb.shape),
            _const_spec(lng.shape), _const_spec(lnb.shape), _const_spec(wpw.shape),
            _const_spec(wout.shape), _const_spec(g2.shape), _const_spec(w1.shape),
            _const_spec(w2.shape),
        ],
        out_specs=tok(d),
        scratch_shapes=[
            pltpu.VMEM((SUBLANES, CONV_GROUP + HALO, CONV_CH), F32),
            pltpu.VMEM((CONV_WIDTH, SUBLANES, CONV_CH), F32),
            pltpu.VMEM((2, t, CONV_CH), BF16),
            pltpu.VMEM((t, d), BF16),
            pltpu.VMEM((t, w1.shape[1]), BF16),
        ],
        compiler_params=pltpu.CompilerParams(
            dimension_semantics=("arbitrary",), vmem_limit_bytes=VMEM_LIMIT_BYTES),
        name="ffn",
    )(flat(x), flat(attn), flat(gates), mod, flat(u), flat(u), flat(u),
      wo, cw, cb, lng, lnb, wpw, wout, g2, w1, w2)
    return out.reshape(bsz, seq, d)


def _head_blocks(w, width):
    return w.reshape(w.shape[0], N_HEADS, width)


def _pad_heads(w3):
    k, h, w = w3.shape
    return jnp.pad(w3, ((0, 0), (0, 0), (0, HEAD_BLOCK - w))).reshape(k, h * HEAD_BLOCK)


def _swap_rope_halves(w3):
    nope = jnp.zeros_like(w3[..., :QK_NOPE_DIM])
    lo = w3[..., QK_NOPE_DIM:QK_NOPE_DIM + ROPE_HALF]
    hi = w3[..., QK_NOPE_DIM + ROPE_HALF:]
    return jnp.concatenate([nope, hi, lo], axis=-1)


def _rope_gain_tables(seq, gain, premul):
    inv_freq = ROPE_THETA ** (-jnp.arange(0, QK_ROPE_DIM, 2, dtype=F32) / QK_ROPE_DIM)
    ang = jnp.arange(seq, dtype=F32)[:, None] * inv_freq[None, :]
    cos, sin = jnp.cos(ang), jnp.sin(ang)
    pad = HEAD_BLOCK - QK_HEAD_DIM
    g_nope = gain[:QK_NOPE_DIM]
    g_lo = gain[QK_NOPE_DIM:QK_NOPE_DIM + ROPE_HALF]
    g_hi = gain[QK_NOPE_DIM + ROPE_HALF:]
    a = jnp.concatenate([jnp.broadcast_to(g_nope, (seq, QK_NOPE_DIM)), cos * g_lo, cos * g_hi,
                         jnp.zeros((seq, pad), F32)], axis=-1)
    b = jnp.concatenate([jnp.zeros((seq, QK_NOPE_DIM), F32), -sin * g_hi, sin * g_lo,
                         jnp.zeros((seq, pad), F32)], axis=-1)
    return a * premul, b * premul


def kernel(x, c, w_ada, b_ada, norm1_g, w_in, q_latent_g, w_uq, kv_latent_g, w_ukv, qk_norm_q_g,
           qk_norm_k_g, w_o_mla, conv_w, conv_b, conv_ln_g, conv_ln_b, w_pw_out, w_out, norm2_g,
           w_ff1, w_ff2):
    bsz, seq, d = x.shape
    depth = w_ada.shape[0]
    off_q = Q_LORA_RANK
    off_kv = off_q + KV_LORA_RANK
    off_kr = off_kv + QK_ROPE_DIM
    off_glu = off_kr + 2 * CONV_CH
    row = lambda v: v.reshape(1, -1)

    q_premul = (QK_HEAD_DIM ** -0.5) * math.log2(math.e)

    for l in range(depth):
        wi = w_in[l]
        kr_cols = wi[:, off_kv:off_kr]
        kr_swapped = jnp.concatenate([kr_cols[:, ROPE_HALF:], kr_cols[:, :ROPE_HALF]], axis=-1)
        wa = jnp.concatenate(
            [wi[:, :off_kr], kr_swapped,
             jnp.zeros((d, 4 * LANES - off_kr - QK_ROPE_DIM), wi.dtype)], axis=-1).astype(BF16)
        wglu = wi[:, off_kr:off_glu].astype(BF16)
        wgate = wi[:, off_glu:].astype(BF16)

        uq3 = _head_blocks(w_uq[l], QK_HEAD_DIM)
        wuq = _pad_heads(uq3).T.astype(BF16)
        wuqs = _pad_heads(_swap_rope_halves(uq3)).T.astype(BF16)

        ukv3 = _head_blocks(w_ukv[l], QK_NOPE_DIM + V_HEAD_DIM)
        wuk = _pad_heads(ukv3[..., :QK_NOPE_DIM]).astype(BF16)
        wuvt = jnp.pad(ukv3[..., QK_NOPE_DIM:], ((0, 0), (0, 0), (0, VT_ROWS - V_HEAD_DIM)))
        wuvt = wuvt.reshape(KV_LORA_RANK, N_HEADS * VT_ROWS).T.astype(BF16)

        aq, bq = (tab.T for tab in _rope_gain_tables(seq, qk_norm_q_g[l], q_premul))
        ak, bk = _rope_gain_tables(seq, qk_norm_k_g[l], 1.0)

        mod = _adaln(c, w_ada[l], b_ada[l]).reshape(bsz, ADA_CHUNKS, d)
        qt, k, vt, u, gates = _proj(
            x, mod, row(norm1_g[l]), wa, wglu, wgate, row(q_latent_g[l]), wuq, wuqs,
            row(kv_latent_g[l]), wuk, wuvt, aq, bq, ak, bk)
        attn = _attn(qt, k, vt)
        x = _ffn(x, attn, u, gates, mod, w_o_mla[l].astype(BF16), conv_w[l], row(conv_b[l]),
                 row(conv_ln_g[l]), row(conv_ln_b[l]), w_pw_out[l].astype(BF16),
                 w_out[l].astype(BF16), row(norm2_g[l]), w_ff1[l].astype(BF16),
                 w_ff2[l].astype(BF16))
    return x
```

```python
import functools
import itertools
import math

import jax
import jax.numpy as jnp
from jax import lax
from jax.experimental import pallas as pl
from jax.experimental.pallas import tpu as pltpu

F32 = jnp.float32
BF16 = jnp.bfloat16

CHUNK = 64
N_HEADS = 8
QK_NOPE_DIM = 64
QK_ROPE_DIM = 32
QK_HEAD_DIM = QK_NOPE_DIM + QK_ROPE_DIM
V_HEAD_DIM = 64
Q_LORA_RANK = 256
KV_LORA_RANK = 128
CONV_CH = 512
CONV_WIDTH = 31
ADA_CHUNKS = 6
ROPE_THETA = 10000.0
EPS = 1e-6

LANES = 128
SUBLANES = 8
VMEM_LIMIT_BYTES = 56 * 1024 * 1024

HEAD_BLOCK = LANES
VT_ROWS = 80
HALO = 32
ROPE_HALF = QK_ROPE_DIM // 2

PROJ_TILE = 512
FFN_TILE = 512
ATTN_QBLOCK = 256
ATTN_KCHUNK = 1024
ATTN_HEADS = 4
ATTN_LOOKAHEAD = 3
CONV_ROWS = 32
CONV_ACC_BLOCKS = 2
CONV_WINDOWS = (3, 4, 4, 4, 1)
CONV_GROUP = 128
FF_CHUNK = 1024


def _sigmoid(x):
    return 0.5 * jnp.tanh(0.5 * x) + 0.5


def _rms_scale(x, width):
    return lax.rsqrt(jnp.sum(x * x, axis=-1, keepdims=True) * (1.0 / width) + EPS)


def _adaln_kernel(c_ref, w_ref, b_ref, o_ref):
    c = c_ref[...]
    act = (c * _sigmoid(c)).astype(BF16)
    o_ref[...] = jnp.dot(act, w_ref[...].astype(BF16), preferred_element_type=F32) + b_ref[...]


def _adaln(c, w, b):
    bsz, d = c.shape
    n = w.shape[1]
    tn = 1024
    return pl.pallas_call(
        _adaln_kernel,
        out_shape=jax.ShapeDtypeStruct((bsz, n), F32),
        grid=(n // tn,),
        in_specs=[
            pl.BlockSpec((bsz, d), lambda j: (0, 0)),
            pl.BlockSpec((d, tn), lambda j: (0, j)),
            pl.BlockSpec((1, tn), lambda j: (0, j)),
        ],
        out_specs=pl.BlockSpec((bsz, tn), lambda j: (0, j)),
        compiler_params=pltpu.CompilerParams(
            dimension_semantics=("arbitrary",), vmem_limit_bytes=VMEM_LIMIT_BYTES),
        name="adaln",
    )(c, w, b.reshape(1, n))


def _proj_kernel(x_ref, mod_ref, g1_ref, wa_ref, wglu_ref, wgate_ref, gq_ref, wuq_ref,
                 gkv_ref, wuk_ref, wuvt_ref, aq_ref, bq_ref, ak_ref, bk_ref,
                 qt_ref, k_ref, vt_ref, u_ref, gate_ref):
    def latents(r):
        x = x_ref[0, r, :]
        scale_row = g1_ref[...] * (1.0 + mod_ref[0, 1:2, :])
        shift_row = mod_ref[0, 0:1, :]
        hb = (x * _rms_scale(x, x.shape[-1]) * scale_row + shift_row).astype(BF16)
        return hb, jnp.dot(hb, wa_ref[...], preferred_element_type=F32)

    def glu(r, hb):
        zg = jnp.dot(hb, wglu_ref[...], preferred_element_type=F32)
        u_ref[0, r, :] = zg[:, :CONV_CH] * _sigmoid(zg[:, CONV_CH:])

    def gates(r, hb):
        gate_ref[0, r, :] = _sigmoid(
            jnp.dot(hb, wgate_ref[...], preferred_element_type=F32)).astype(BF16)

    def heads(r, z_a):
        zq = z_a[:, :Q_LORA_RANK]
        qn = (zq * _rms_scale(zq, Q_LORA_RANK) * gq_ref[...]).astype(BF16)
        contract_last = (((1,), (1,)), ((), ()))
        q_pre = lax.dot_general(wuq_ref[...], qn, contract_last, preferred_element_type=F32)
        lo = slice(QK_NOPE_DIM, QK_NOPE_DIM + ROPE_HALF)
        hi = slice(QK_NOPE_DIM + ROPE_HALF, QK_HEAD_DIM)
        aq = aq_ref[:, r]
        bq = bq_ref[:, r]
        for h in range(N_HEADS):
            sl = slice(h * HEAD_BLOCK, (h + 1) * HEAD_BLOCK)
            blk = q_pre[sl, :]
            ss = jnp.sum(blk * blk, axis=0, keepdims=True)
            swp = jnp.concatenate([jnp.zeros_like(blk[:QK_NOPE_DIM]), blk[hi], blk[lo],
                                   jnp.zeros_like(blk[QK_HEAD_DIM:])], axis=0)
            qh = lax.rsqrt(ss * (1.0 / QK_HEAD_DIM) + EPS) * (blk * aq + swp * bq)
            qt_ref[0, sl, r] = qh.astype(BF16)

        zkv = z_a[:, Q_LORA_RANK:Q_LORA_RANK + KV_LORA_RANK]
        kvn = (zkv * _rms_scale(zkv, KV_LORA_RANK) * gkv_ref[...]).astype(BF16)
        k_up = jnp.dot(kvn, wuk_ref[...], preferred_element_type=F32)

        kr_blk = z_a[:, Q_LORA_RANK + KV_LORA_RANK:]
        lane = lax.broadcasted_iota(jnp.int32, (1, LANES), 1)
        kr_only = jnp.where(lane < QK_ROPE_DIM, kr_blk, 0.0)
        ss_kr = jnp.sum(kr_only * kr_only, axis=-1, keepdims=True)
        ak = ak_ref[r, :]
        bk = bk_ref[r, :]
        kr_rot = (pltpu.roll(kr_blk, QK_NOPE_DIM, 1) * ak
                  + pltpu.roll(kr_blk, QK_NOPE_DIM - QK_ROPE_DIM, 1) * bk)
        for h in range(N_HEADS):
            sl = slice(h * HEAD_BLOCK, (h + 1) * HEAD_BLOCK)
            blk = k_up[:, sl]
            ss = jnp.sum(blk * blk, axis=-1, keepdims=True) + ss_kr
            kh = lax.rsqrt(ss * (1.0 / QK_HEAD_DIM) + EPS) * (blk * ak + kr_rot)
            k_ref[0, r, sl] = kh.astype(BF16)

        vt = lax.dot_general(wuvt_ref[...], kvn, contract_last, preferred_element_type=F32)
        row = lax.broadcasted_iota(jnp.int32, vt.shape, 0)
        vt_ref[0, :, r] = jnp.where(row % VT_ROWS == V_HEAD_DIM, 1.0, vt).astype(BF16)

    t = x_ref.shape[1]
    ra, rb = slice(0, t // 2), slice(t // 2, t)
    hb_a, z_a = latents(ra)
    glu(ra, hb_a)
    hb_b, z_b = latents(rb)
    gates(ra, hb_a)
    heads(ra, z_a)
    glu(rb, hb_b)
    heads(rb, z_b)
    gates(rb, hb_b)


def _const_spec(shape):
    nd = len(shape)
    return pl.BlockSpec(shape, lambda *_: (0,) * nd, pipeline_mode=pl.Buffered(1))


def _proj(x, mod, g1, wa, wglu, wgate, gq, wuq, gkv, wuk, wuvt, aq, bq, ak, bk):
    bsz, seq, d = x.shape
    t = PROJ_TILE
    d_qk = N_HEADS * HEAD_BLOCK
    vt_rows = N_HEADS * VT_ROWS
    tok = lambda w: pl.BlockSpec((1, t, w), lambda b, s: (b, s, 0))
    pos = lambda: pl.BlockSpec((t, LANES), lambda b, s: (s, 0))
    pos_t = lambda: pl.BlockSpec((LANES, t), lambda b, s: (0, s))
    by_rows = lambda n: pl.BlockSpec((1, n, t), lambda b, s: (b, 0, s))
    out_shapes = (
        jax.ShapeDtypeStruct((bsz, d_qk, seq), BF16),
        jax.ShapeDtypeStruct((bsz, seq, d_qk), BF16),
        jax.ShapeDtypeStruct((bsz, vt_rows, seq), BF16),
        jax.ShapeDtypeStruct((bsz, seq, CONV_CH), F32),
        jax.ShapeDtypeStruct((bsz, seq, 2 * d), BF16),
    )
    return pl.pallas_call(
        _proj_kernel,
        out_shape=out_shapes,
        grid=(bsz, seq // t),
        in_specs=[
            tok(d),
            pl.BlockSpec((1, ADA_CHUNKS, d), lambda b, s: (b, 0, 0)),
            _const_spec(g1.shape), _const_spec(wa.shape), _const_spec(wglu.shape),
            _const_spec(wgate.shape), _const_spec(gq.shape), _const_spec(wuq.shape),
            _const_spec(gkv.shape), _const_spec(wuk.shape),
            _const_spec(wuvt.shape), pos_t(), pos_t(), pos(), pos(),
        ],
        out_specs=(by_rows(d_qk), tok(d_qk), by_rows(vt_rows), tok(CONV_CH), tok(2 * d)),
        compiler_params=pltpu.CompilerParams(
            dimension_semantics=("arbitrary", "arbitrary"), vmem_limit_bytes=VMEM_LIMIT_BYTES),
        name="proj",
    )(x, mod, g1, wa, wglu, wgate, gq, wuq, gkv, wuk, wuvt, aq, bq, ak, bk)


def _attn_kernel(qt_ref, k_ref, vt_ref, o_ref):
    seq = k_ref.shape[1]
    qb = ATTN_QBLOCK
    neg = jnp.finfo(F32).min
    key_chunk = lax.broadcasted_iota(jnp.int32, (qb, qb), 0) // CHUNK
    query_chunk = lax.broadcasted_iota(jnp.int32, (qb, qb), 1) // CHUNK
    diag_mask = key_chunk <= query_chunk
    heads = o_ref.shape[2] // V_HEAD_DIM
    blocks = [(j, hh) for j in range(seq // qb) for hh in range(heads)]

    def key_chunks(j):
        q0 = j * qb
        edges = list(range(0, q0, ATTN_KCHUNK)) + [q0]
        return [(a, b, False) for a, b in zip(edges[:-1], edges[1:])] + [(q0, q0 + qb, True)]

    def score_steps(j, hh, out):
        sl = slice(hh * HEAD_BLOCK, (hh + 1) * HEAD_BLOCK)
        qt = qt_ref[0, sl, j * qb:(j + 1) * qb]
        m = None
        for a, b, diag in key_chunks(j):
            s = jnp.dot(k_ref[0, a:b, sl], qt, preferred_element_type=F32)
            if diag:
                s = jnp.where(diag_mask, s, neg)
            cm = jnp.max(s, axis=0, keepdims=True)
            m = cm if m is None else jnp.maximum(m, cm)
            out.append(s)
            yield
        out.append(m)

    def value_steps(j, hh, sc, out):
        rows = slice(hh * VT_ROWS, (hh + 1) * VT_ROWS)
        m = sc[-1]
        ot = None
        for (a, b, _), s in zip(key_chunks(j), sc[:-1]):
            part = jnp.dot(vt_ref[0, rows, a:b], jnp.exp2(s - m).astype(BF16),
                           preferred_element_type=F32)
            ot = part if ot is None else ot + part
            yield
        out.append(ot[:V_HEAD_DIM] / ot[V_HEAD_DIM:V_HEAD_DIM + 1])

    halves = []
    ahead = ATTN_LOOKAHEAD
    queued = []
    for n in range(min(ahead, len(blocks))):
        sc = []
        for _ in score_steps(*blocks[n], sc):
            pass
        queued.append(sc)
    for n, (j, hh) in enumerate(blocks):
        current = queued.pop(0)
        pending = []
        nxt = score_steps(*blocks[n + ahead], pending) if n + ahead < len(blocks) else iter(())
        cur = value_steps(j, hh, current, halves)
        live = [nxt, cur]
        while live:
            for it in list(live):
                if next(it, StopIteration) is StopIteration:
                    live.remove(it)
        if n + ahead < len(blocks):
            queued.append(pending)
        if hh == heads - 1:
            o_ref[0, j * qb:(j + 1) * qb, :] = jnp.concatenate(halves, axis=0).T.astype(BF16)
            halves = []


def _attn(qt, k, vt):
    bsz, seq, d_qk = k.shape
    group = ATTN_HEADS * HEAD_BLOCK
    spec = pl.BlockSpec((1, seq, group), lambda b, p: (b, 0, p))
    by_rows = lambda n: pl.BlockSpec((1, n, seq), lambda b, p: (b, p, 0))
    return pl.pallas_call(
        _attn_kernel,
        out_shape=jax.ShapeDtypeStruct((bsz, seq, N_HEADS * V_HEAD_DIM), BF16),
        grid=(bsz, d_qk // group),
        in_specs=[by_rows(group), spec, by_rows(ATTN_HEADS * VT_ROWS)],
        out_specs=pl.BlockSpec((1, seq, ATTN_HEADS * V_HEAD_DIM), lambda b, p: (b, 0, p)),
        compiler_params=pltpu.CompilerParams(
            dimension_semantics=("arbitrary", "arbitrary"), vmem_limit_bytes=VMEM_LIMIT_BYTES),
        name="attn",
    )(qt, k, vt)


def _conv_group(*args, **kwargs):
    for _ in _conv_group_steps(*args, **kwargs):
        pass


def _conv_group_steps(g, u_ref, halo, out_ref, ubuf, wbc, cb_ref, lng_ref, lnb_ref,
                      before_chunk=None, after_chunk=None):
    if isinstance(g, int):
        g0 = g * CONV_GROUP
        history = halo if g == 0 else u_ref[g0 - HALO:g0, :]
    else:
        g0 = pl.multiple_of(g * CONV_GROUP, CONV_GROUP)
        prev0 = pl.multiple_of(jnp.maximum(g0 - HALO, 0), HALO)
        history = jnp.where(g == 0, halo, u_ref[pl.ds(prev0, HALO), :])
    ext = CONV_GROUP + HALO
    ubuf[0, 0:HALO, :] = history
    ubuf[0, HALO:, :] = u_ref[pl.ds(g0, CONV_GROUP), :]
    for p in range(1, SUBLANES):
        ubuf[p, 0:ext - SUBLANES, :] = ubuf[0, p:ext - SUBLANES + p, :]

    tap0 = HALO - (CONV_WIDTH - 1)
    blocks = CONV_ROWS // SUBLANES
    for c in range(CONV_GROUP // CONV_ROWS):
        r0 = c * CONV_ROWS
        bias = jnp.broadcast_to(cb_ref[...], (SUBLANES, CONV_CH))
        token = None if before_chunk is None else before_chunk(c)
        if token is not None:
            bias = bias + jnp.concatenate([_unknown_zero(token)] * (CONV_CH // LANES), axis=1)
        acc = [bias] * blocks
        for rb0 in range(0, blocks, CONV_ACC_BLOCKS):
            for tap in range(CONV_WIDTH):
                blk, phase = divmod(tap0 + tap, SUBLANES)
                w = wbc[tap]
                for rb in range(rb0, rb0 + CONV_ACC_BLOCKS):
                    row = r0 + (blk + rb) * SUBLANES
                    acc[rb] = acc[rb] + ubuf[phase, row:row + SUBLANES, :] * w
        conv = jnp.concatenate(acc, axis=0)
        cen = conv - jnp.mean(conv, axis=-1, keepdims=True)
        var = jnp.mean(cen * cen, axis=-1, keepdims=True)
        y = cen * lax.rsqrt(var + EPS) * lng_ref[...] + lnb_ref[...]
        out_ref[pl.ds(g0 + r0, CONV_ROWS), :] = (y * _sigmoid(y)).astype(BF16)
        if after_chunk is not None:
            after_chunk(c, y[0:SUBLANES, 0:LANES])
        yield


def _unknown_zero(tile):
    sixteen = jnp.uint32(16)
    bits = pltpu.bitcast(tile, jnp.uint32)
    zero_bits = lax.shift_right_logical(lax.shift_right_logical(bits, sixteen), sixteen)
    return pltpu.bitcast(zero_bits, F32)


def _order_after(ref, rows, tile):
    zero = _unknown_zero(tile)
    keep = ref[rows, 0:LANES].astype(F32) + jnp.concatenate([zero, zero], axis=0)
    ref[rows, 0:LANES] = keep.astype(BF16)


def _ffn_kernel(x_ref, attn_ref, gate_ref, mod_ref, ufirst_ref, unext_ref, halo_ref,
                wo_ref, cw_ref, cb_ref, lng_ref, lnb_ref, wpw_ref, wout_ref, g2_ref, w1_ref, w2_ref,
                o_ref, ubuf, wbc, cbuf, h2_ref, hid_ref, *, tiles_per_seq):
    t, d = x_ref.shape
    i = pl.program_id(0)
    slot = i % 2
    groups = t // CONV_GROUP
    conv_args = (ubuf, wbc, cb_ref, lng_ref, lnb_ref)

    @pl.when(i == 0)
    def _():
        for tap in range(CONV_WIDTH):
            wbc[tap] = jnp.broadcast_to(cw_ref[tap:tap + 1, :], (SUBLANES, CONV_CH))
        zero_halo = jnp.zeros((HALO, CONV_CH), F32)

        def first_tile(g, carry):
            _conv_group(g, ufirst_ref, zero_halo, cbuf.at[0], *conv_args)
            return carry

        lax.fori_loop(0, groups, first_tile, 0)

    half = t // 2
    row_halves = [slice(0, half), slice(half, t)]

    y_a = [jnp.dot(attn_ref[r, :], wo_ref[...], preferred_element_type=F32) for r in row_halves]
    y_b = [jnp.dot(cbuf[slot, r, :], wpw_ref[...], preferred_element_type=F32) for r in row_halves]
    mixed = []
    for r, ya, yb in zip(row_halves, y_a, y_b):
        mix = (gate_ref[r, :d].astype(F32) * ya + gate_ref[r, d:].astype(F32) * yb).astype(BF16)
        mixed.append(jnp.dot(mix, wout_ref[...], preferred_element_type=F32))

    next_starts_seq = (i + 1) % tiles_per_seq == 0
    halo_next = jnp.where(next_starts_seq, 0.0, halo_ref[...])

    pin_rows = [slice(r.start, r.start + 2 * SUBLANES) for r in row_halves]
    window_ends = list(itertools.accumulate(CONV_WINDOWS))
    window_pins = [(h2_ref, pin_rows[0]), (h2_ref, pin_rows[1]),
                   (hid_ref, pin_rows[0]), (hid_ref, pin_rows[1]), None]
    state = {"token": None, "done": 0}

    def before_chunk(c):
        return state["token"]

    def after_chunk(c, tile):
        state["done"] += 1
        if state["done"] in window_ends:
            pin = window_pins[window_ends.index(state["done"])]
            if pin is not None:
                _order_after(*pin, tile)

    def conv_steps():
        for g in range(groups):
            yield from _conv_group_steps(g, unext_ref, halo_next, cbuf.at[1 - slot], *conv_args,
                                         before_chunk=before_chunk, after_chunk=after_chunk)

    conv = conv_steps()

    def run_window(k, token):
        state["token"] = token
        while state["done"] < window_ends[k]:
            next(conv)

    def up_phase(r):
        for c in range(0, w1_ref.shape[1], FF_CHUNK):
            cols = slice(c, c + FF_CHUNK)
            up = jnp.maximum(jnp.dot(h2_ref[r, :], w1_ref[:, cols], preferred_element_type=F32), 0.0)
            hid_ref[r, cols] = (up * up).astype(BF16)
        return up[0:SUBLANES, 0:LANES]

    scale_row = g2_ref[...] * (1.0 + mod_ref[0, 4:5, :])
    for r, mx in zip(row_halves, mixed):
        x1 = x_ref[r, :] + mod_ref[0, 2:3, :] * mx
        o_ref[r, :] = x1
        h2_ref[r, :] = (x1 * _rms_scale(x1, d) * scale_row + mod_ref[0, 3:4, :]).astype(BF16)

    run_window(0, None)
    run_window(1, mixed[1][0:SUBLANES, 0:LANES])
    token = up_phase(row_halves[0])
    run_window(2, token)
    token = up_phase(row_halves[1])
    run_window(3, token)
    for n, r in enumerate(row_halves):
        down = jnp.dot(hid_ref[r, :], w2_ref[...], preferred_element_type=F32)
        o_ref[r, :] = o_ref[r, :] + mod_ref[0, 5:6, :] * down
        if n == 0:
            run_window(4, down[0:SUBLANES, 0:LANES])


def _ffn(x, attn, u, gates, mod, wo, cw, cb, lng, lnb, wpw, wout, g2, w1, w2):
    bsz, seq, d = x.shape
    t = FFN_TILE
    tiles_per_seq = seq // t
    n = bsz * tiles_per_seq
    rows = bsz * seq
    flat = lambda a: a.reshape(rows, a.shape[-1])
    tok = lambda w: pl.BlockSpec((t, w), lambda i: (i, 0))
    next_tile = lambda i: jnp.minimum(i + 1, n - 1)
    kern = functools.partial(_ffn_kernel, tiles_per_seq=tiles_per_seq)
    out = pl.pallas_call(
        kern,
        out_shape=jax.ShapeDtypeStruct((rows, d), x.dtype),
        grid=(n,),
        in_specs=[
            tok(d), tok(attn.shape[-1]), tok(2 * d),
            pl.BlockSpec((1, ADA_CHUNKS, d), lambda i: (i // tiles_per_seq, 0, 0)),
            pl.BlockSpec((t, CONV_CH), lambda i: (0, 0), pipeline_mode=pl.Buffered(1)),
            pl.BlockSpec((t, CONV_CH), lambda i: (next_tile(i), 0)),
            pl.BlockSpec((HALO, CONV_CH), lambda i: (next_tile(i) * (t // HALO) - 1, 0)),
            _const_spec(wo.shape), _const_spec(cw.shape), _const_spec(cb.shape),
            _const_spec(lng.shape), _const_spec(lnb.shape), _const_spec(wpw.shape),
            _const_spec(wout.shape), _const_spec(g2.shape), _const_spec(w1.shape),
            _const_spec(w2.shape),
        ],
        out_specs=tok(d),
        scratch_shapes=[
            pltpu.VMEM((SUBLANES, CONV_GROUP + HALO, CONV_CH), F32),
            pltpu.VMEM((CONV_WIDTH, SUBLANES, CONV_CH), F32),
            pltpu.VMEM((2, t, CONV_CH), BF16),
            pltpu.VMEM((t, d), BF16),
            pltpu.VMEM((t, w1.shape[1]), BF16),
        ],
        compiler_params=pltpu.CompilerParams(
            dimension_semantics=("arbitrary",), vmem_limit_bytes=VMEM_LIMIT_BYTES),
        name="ffn",
    )(flat(x), flat(attn), flat(gates), mod, flat(u), flat(u), flat(u),
      wo, cw, cb, lng, lnb, wpw, wout, g2, w1, w2)
    return out.reshape(bsz, seq, d)


def _head_blocks(w, width):
    return w.reshape(w.shape[0], N_HEADS, width)


def _pad_heads(w3):
    k, h, w = w3.shape
    return jnp.pad(w3, ((0, 0), (0, 0), (0, HEAD_BLOCK - w))).reshape(k, h * HEAD_BLOCK)


def _rope_gain_tables(seq, gain, premul):
    inv_freq = ROPE_THETA ** (-jnp.arange(0, QK_ROPE_DIM, 2, dtype=F32) / QK_ROPE_DIM)
    ang = jnp.arange(seq, dtype=F32)[:, None] * inv_freq[None, :]
    cos, sin = jnp.cos(ang), jnp.sin(ang)
    pad = HEAD_BLOCK - QK_HEAD_DIM
    g_nope = gain[:QK_NOPE_DIM]
    g_lo = gain[QK_NOPE_DIM:QK_NOPE_DIM + ROPE_HALF]
    g_hi = gain[QK_NOPE_DIM + ROPE_HALF:]
    a = jnp.concatenate([jnp.broadcast_to(g_nope, (seq, QK_NOPE_DIM)), cos * g_lo, cos * g_hi,
                         jnp.zeros((seq, pad), F32)], axis=-1)
    b = jnp.concatenate([jnp.zeros((seq, QK_NOPE_DIM), F32), -sin * g_hi, sin * g_lo,
                         jnp.zeros((seq, pad), F32)], axis=-1)
    return a * premul, b * premul


def kernel(x, c, w_ada, b_ada, norm1_g, w_in, q_latent_g, w_uq, kv_latent_g, w_ukv, qk_norm_q_g,
           qk_norm_k_g, w_o_mla, conv_w, conv_b, conv_ln_g, conv_ln_b, w_pw_out, w_out, norm2_g,
           w_ff1, w_ff2):
    bsz, seq, d = x.shape
    depth = w_ada.shape[0]
    off_q = Q_LORA_RANK
    off_kv = off_q + KV_LORA_RANK
    off_kr = off_kv + QK_ROPE_DIM
    off_glu = off_kr + 2 * CONV_CH
    row = lambda v: v.reshape(1, -1)

    q_premul = (QK_HEAD_DIM ** -0.5) * math.log2(math.e)

    for l in range(depth):
        wi = w_in[l]
        kr_cols = wi[:, off_kv:off_kr]
        kr_swapped = jnp.concatenate([kr_cols[:, ROPE_HALF:], kr_cols[:, :ROPE_HALF]], axis=-1)
        wa = jnp.concatenate(
            [wi[:, :off_kr], kr_swapped,
             jnp.zeros((d, 4 * LANES - off_kr - QK_ROPE_DIM), wi.dtype)], axis=-1).astype(BF16)
        wglu = wi[:, off_kr:off_glu].astype(BF16)
        wgate = wi[:, off_glu:].astype(BF16)

        uq3 = _head_blocks(w_uq[l], QK_HEAD_DIM)
        wuq = _pad_heads(uq3).T.astype(BF16)

        ukv3 = _head_blocks(w_ukv[l], QK_NOPE_DIM + V_HEAD_DIM)
        wuk = _pad_heads(ukv3[..., :QK_NOPE_DIM]).astype(BF16)
        wuvt = jnp.pad(ukv3[..., QK_NOPE_DIM:], ((0, 0), (0, 0), (0, VT_ROWS - V_HEAD_DIM)))
        wuvt = wuvt.reshape(KV_LORA_RANK, N_HEADS * VT_ROWS).T.astype(BF16)

        aq, bq = (tab.T for tab in _rope_gain_tables(seq, qk_norm_q_g[l], q_premul))
        ak, bk = _rope_gain_tables(seq, qk_norm_k_g[l], 1.0)

        mod = _adaln(c, w_ada[l], b_ada[l]).reshape(bsz, ADA_CHUNKS, d)
        qt, k, vt, u, gates = _proj(
            x, mod, row(norm1_g[l]), wa, wglu, wgate, row(q_latent_g[l]), wuq,
            row(kv_latent_g[l]), wuk, wuvt, aq, bq, ak, bk)
        attn = _attn(qt, k, vt)
        x = _ffn(x, attn, u, gates, mod, w_o_mla[l].astype(BF16), conv_w[l], row(conv_b[l]),
                 row(conv_ln_g[l]), row(conv_ln_b[l]), w_pw_out[l].astype(BF16),
                 w_out[l].astype(BF16), row(norm2_g[l]), w_ff1[l].astype(BF16),
                 w_ff2[l].astype(BF16))
    return x
```
